```python
import math
import jax, jax.numpy as jnp
from jax import lax
import numpy as np

D_MODEL = 1024
BATCH = 1
SEQ = 16384
DEPTH = 1

CHUNK = 64
MEM_LEN = 256
HEAD_DIM = 64
POOL_WINDOWS = (2, 4, 8, 16)
POOL_GROUPS = len(POOL_WINDOWS)
POOL_GROUP_DIM = 64
POOL_WIDTH = POOL_GROUPS * POOL_GROUP_DIM
ATTN_HEADS = 8
ATTN_WIDTH = ATTN_HEADS * HEAD_DIM
BAND_CHUNKS = 9
BAND_KEYS = BAND_CHUNKS * CHUNK
MAX_REL = 128
MEM_HEADS = 4
MEM_WIDTH = MEM_HEADS * HEAD_DIM
N_BRANCH = 3
IN_COLS = POOL_WIDTH + 3 * ATTN_WIDTH + MEM_WIDTH + N_BRANCH * D_MODEL
D_FF = 2816
CONV_WIDTH = 3
RMS_EPS = 1e-6
NEG_INF = -1e30

kernel_name = "hybrid_pool_chunkattn_mem_convffn"


def rms_norm(x, g):
    xf = x.astype(jnp.float32)
    y = xf * lax.rsqrt(jnp.mean(xf * xf, axis=-1, keepdims=True) + RMS_EPS)
    return (y * g.astype(jnp.float32)).astype(x.dtype)


def pool_mixer(u, w_pool, pool_scale):
    B, S, _ = u.shape
    ug = u.reshape(B, S, POOL_GROUPS, POOL_GROUP_DIM)
    pos1 = jnp.arange(1, S + 1, dtype=jnp.int32)
    outs = []
    for g, w in enumerate(POOL_WINDOWS):
        ui = ug[:, :, g].astype(jnp.float32)
        c = jnp.cumsum(ui, axis=1)
        c_prev = jnp.pad(c, ((0, 0), (w, 0), (0, 0)))[:, :S]
        cnt = jnp.minimum(pos1, w).astype(jnp.float32)[None, :, None]
        outs.append((c - c_prev) / cnt - ui)
    p = jnp.stack(outs, axis=2).astype(u.dtype)
    y = jnp.einsum('bsgc,gcd->bsgd', p, w_pool)
    return y.reshape(B, S, POOL_WIDTH) * pool_scale


def chunk_band_attention(q, k, v, rel_bias):
    B, S, _ = q.shape
    NC = S // CHUNK
    shp = (B, NC, CHUNK, ATTN_HEADS, HEAD_DIM)
    q, k, v = q.reshape(shp), k.reshape(shp), v.reshape(shp)
    pad = ((0, 0), (BAND_CHUNKS - 1, 0), (0, 0), (0, 0), (0, 0))
    kp, vp = jnp.pad(k, pad), jnp.pad(v, pad)
    kb = jnp.stack([kp[:, j:j + NC] for j in range(BAND_CHUNKS)], axis=2).reshape(B, NC, BAND_KEYS, ATTN_HEADS, HEAD_DIM)
    vb = jnp.stack([vp[:, j:j + NC] for j in range(BAND_CHUNKS)], axis=2).reshape(B, NC, BAND_KEYS, ATTN_HEADS, HEAD_DIM)
    s = jnp.einsum('bnqhd,bnkhd->bnhqk', q, kb, preferred_element_type=jnp.float32) * (HEAD_DIM ** -0.5)
    qi = jnp.arange(CHUNK)
    kj = jnp.arange(BAND_KEYS)
    rel = qi[:, None] + (BAND_CHUNKS - 1) * CHUNK - kj[None, :]
    idx = jnp.clip(rel, -MAX_REL, MAX_REL) + MAX_REL
    bias = rel_bias[:, idx].astype(jnp.float32)
    s = s + bias[None, None]
    valid = (jnp.arange(NC)[:, None] - (BAND_CHUNKS - 1) + kj[None, :] // CHUNK) >= 0
    s = jnp.where(valid[None, :, None, None, :], s, NEG_INF)
    p = jax.nn.softmax(s, axis=-1).astype(v.dtype)
    o = jnp.einsum('bnhqk,bnkhd->bnqhd', p, vb)
    return o.reshape(B, S, ATTN_WIDTH)


def memory_attention(q, mem_n, w_mem_kv):
    B, S, _ = q.shape
    M = mem_n.shape[1]
    kv = (mem_n @ w_mem_kv).reshape(B, M, 2, MEM_HEADS, HEAD_DIM)
    km, vm = kv[:, :, 0], kv[:, :, 1]
    qh = q.reshape(B, S, MEM_HEADS, HEAD_DIM)
    s = jnp.einsum('bshd,bmhd->bhsm', qh, km, preferred_element_type=jnp.float32) * (HEAD_DIM ** -0.5)
    p = jax.nn.softmax(s, axis=-1).astype(vm.dtype)
    o = jnp.einsum('bhsm,bmhd->bshd', p, vm)
    return o.reshape(B, S, MEM_WIDTH)


def conv_gated_mlp(h, w_up, conv_w, conv_b, w_down):
    S = h.shape[1]
    a = h @ w_up
    ap = jnp.pad(a, ((0, 0), (CONV_WIDTH - 1, 0), (0, 0)))
    c = conv_b + sum(ap[:, t:t + S] * conv_w[t] for t in range(CONV_WIDTH))
    gate, val = c[..., :D_FF], c[..., D_FF:]
    return (jax.nn.gelu(gate, approximate=False) * val) @ w_down


def setup_inputs(seed: int = 0) -> dict:
    key = jax.random.key(seed)
    ks = jax.random.split(key, 24)
    f32 = jnp.float32
    nrm = lambda k, shape, fan_in: jax.random.normal(k, shape, f32) * (fan_in ** -0.5)
    gain = lambda k, shape: 1.0 + 0.05 * jax.random.normal(k, shape, f32)
    L = DEPTH
    return {
        "x": jax.random.normal(ks[0], (BATCH, SEQ, D_MODEL), f32),
        "mem": jax.random.normal(ks[1], (BATCH, MEM_LEN, D_MODEL), f32),
        "norm_mix_g": gain(ks[2], (L, D_MODEL)),
        "norm_mem_g": gain(ks[3], (L, D_MODEL)),
        "w_in": nrm(ks[4], (L, D_MODEL, IN_COLS), D_MODEL),
        "b_gate": 0.02 * jax.random.normal(ks[5], (L, N_BRANCH * D_MODEL), f32),
        "w_pool": nrm(ks[6], (L, POOL_GROUPS, POOL_GROUP_DIM, POOL_GROUP_DIM), POOL_GROUP_DIM),
        "pool_scale": gain(ks[7], (L, POOL_WIDTH)),
        "rel_bias": 0.5 * jax.random.normal(ks[8], (L, ATTN_HEADS, 2 * MAX_REL + 1), f32),
        "w_mem_kv": nrm(ks[9], (L, D_MODEL, 2 * MEM_WIDTH), D_MODEL),
        "w_up_pool": nrm(ks[10], (L, POOL_WIDTH, D_MODEL), POOL_WIDTH),
        "w_up_attn": nrm(ks[11], (L, ATTN_WIDTH, D_MODEL), ATTN_WIDTH),
        "w_up_mem": nrm(ks[12], (L, MEM_WIDTH, D_MODEL), MEM_WIDTH),
        "w_out": nrm(ks[13], (L, D_MODEL, D_MODEL), D_MODEL),
        "norm_ffn_g": gain(ks[14], (L, D_MODEL)),
        "w_ffn_up": nrm(ks[15], (L, D_MODEL, 2 * D_FF), D_MODEL),
        "conv_w": nrm(ks[16], (L, CONV_WIDTH, 2 * D_FF), CONV_WIDTH),
        "conv_b": 0.02 * jax.random.normal(ks[17], (L, 2 * D_FF), f32),
        "w_ffn_down": nrm(ks[18], (L, D_FF, D_MODEL), D_FF),
        "norm_final_g": gain(ks[19], (D_MODEL,)),
    }


def reference(x, mem, norm_mix_g, norm_mem_g, w_in, b_gate, w_pool, pool_scale, rel_bias,
              w_mem_kv, w_up_pool, w_up_attn, w_up_mem, w_out, norm_ffn_g, w_ffn_up,
              conv_w, conv_b, w_ffn_down, norm_final_g):
    B, S, D = x.shape
    o1 = POOL_WIDTH
    o2 = o1 + ATTN_WIDTH
    o3 = o2 + ATTN_WIDTH
    o4 = o3 + ATTN_WIDTH
    o5 = o4 + MEM_WIDTH
    for l in range(DEPTH):
        h = rms_norm(x, norm_mix_g[l])
        z = h @ w_in[l]
        u_pool = z[..., :o1]
        q_a, k_a, v_a = z[..., o1:o2], z[..., o2:o3], z[..., o3:o4]
        q_m = z[..., o4:o5]
        gates = jax.nn.sigmoid(z[..., o5:] + b_gate[l]).reshape(B, S, N_BRANCH, D)

        y_pool = pool_mixer(u_pool, w_pool[l], pool_scale[l]) @ w_up_pool[l]
        y_attn = chunk_band_attention(q_a, k_a, v_a, rel_bias[l]) @ w_up_attn[l]
        mem_n = rms_norm(mem, norm_mem_g[l])
        y_mem = memory_attention(q_m, mem_n, w_mem_kv[l]) @ w_up_mem[l]

        merged = gates[:, :, 0] * y_pool + gates[:, :, 1] * y_attn + gates[:, :, 2] * y_mem
        x = x + merged @ w_out[l]

        h2 = rms_norm(x, norm_ffn_g[l])
        x = x + conv_gated_mlp(h2, w_ffn_up[l], conv_w[l], conv_b[l], w_ffn_down[l])
    return rms_norm(x, norm_final_g)
```

```python
import functools

import jax
import jax.numpy as jnp
from jax import lax
from jax.experimental import pallas as pl
from jax.experimental.pallas import tpu as pltpu

D_MODEL = 1024
CHUNK = 64
HEAD_DIM = 64
POOL_WINDOWS = (2, 4, 8, 16)
POOL_GROUP_DIM = 64
POOL_WIDTH = len(POOL_WINDOWS) * POOL_GROUP_DIM
ATTN_HEADS = 8
ATTN_WIDTH = ATTN_HEADS * HEAD_DIM
BAND_CHUNKS = 9
BAND_KEYS = BAND_CHUNKS * CHUNK
MAX_REL = 128
MEM_HEADS = 4
MEM_WIDTH = MEM_HEADS * HEAD_DIM
N_BRANCH = 3
D_FF = 2816
CONV_WIDTH = 3
RMS_EPS = 1e-6
NEG_INF = -1e30

O_POOL = 0
O_Q = O_POOL + POOL_WIDTH
O_K = O_Q + ATTN_WIDTH
O_V = O_K + ATTN_WIDTH
O_QM = O_V + ATTN_WIDTH
O_GATE = O_QM + MEM_WIDTH

LANES = 128
SUBLANES = 8
PAIR = 2 * HEAD_DIM
POOL_TAIL = 16
WIN_KEYS = BAND_KEYS + CHUNK
KV_CARRY = WIN_KEYS - CHUNK
SEQ_TILE = 512
MEM_ROWS = 128
FF_BLOCK = 256
CONV_PAD = SUBLANES
VMEM_LIMIT = 56 * 1024 * 1024


def _rms_norm(x, g):
    y = x * lax.rsqrt(jnp.mean(x * x, axis=-1, keepdims=True) + RMS_EPS)
    return y * g


def _dot(a, b):
    return jnp.dot(a, b, preferred_element_type=jnp.float32)


def _dot_nt(a, b):
    return lax.dot_general(a, b, (((1,), (1,)), ((), ())), preferred_element_type=jnp.float32)


def _softmax_pv(s, v):
    m = jnp.max(s, axis=-1, keepdims=True)
    e = jnp.exp(s - m)
    l = jnp.sum(e, axis=-1, keepdims=True)
    return _dot(e.astype(jnp.bfloat16), v) / l


def _mem_kv_kernel(mem_ref, g_ref, w_ref, k_ref, v_ref):
    mem_n = _rms_norm(mem_ref[...], g_ref[...]).astype(jnp.bfloat16)
    kv = _dot(mem_n, w_ref[...])
    k_ref[...] = kv[:, :MEM_WIDTH].astype(jnp.bfloat16)
    v_ref[...] = kv[:, MEM_WIDTH:].astype(jnp.bfloat16)


def _mixer_kernel(x_ref, g_ref, w_in_ref, b_gate_ref, w_pool_ref, pool_scale_ref, bias_ref,
                  km_ref, vm_ref, w_up_pool_ref, w_up_attn_ref, w_up_mem_ref, w_out_ref,
                  o_ref,
                  h_buf, u_buf, q_buf, k_buf, v_buf, qm_buf, pool_buf, attn_buf, mem_buf, merged_buf):
    ts = x_ref.shape[0]
    tile = pl.program_id(0)

    @pl.when(tile == 0)
    def _init():
        u_buf[0:POOL_TAIL, :] = jnp.zeros((POOL_TAIL, POOL_WIDTH), jnp.float32)
        k_buf[0:KV_CARRY, :] = jnp.zeros((KV_CARRY, ATTN_WIDTH), jnp.bfloat16)
        v_buf[0:KV_CARRY, :] = jnp.zeros((KV_CARRY, ATTN_WIDTH), jnp.bfloat16)

    h_buf[...] = _rms_norm(x_ref[...], g_ref[...]).astype(jnp.bfloat16)

    u_buf[POOL_TAIL:POOL_TAIL + ts, :] = _dot(h_buf[...], w_in_ref[:, O_POOL:O_Q])
    def split_heads(q, even_ref, odd_ref):
        odd = (lax.broadcasted_iota(jnp.int32, q.shape, 1) // HEAD_DIM) % 2 == 1
        q = q * (HEAD_DIM ** -0.5)
        even_ref[...] = jnp.where(odd, 0.0, q).astype(jnp.bfloat16)
        odd_ref[...] = jnp.where(odd, q, 0.0).astype(jnp.bfloat16)

    split_heads(_dot(h_buf[...], w_in_ref[:, O_Q:O_K]), q_buf.at[0], q_buf.at[1])
    k_buf[KV_CARRY:KV_CARRY + ts, :] = _dot(h_buf[...], w_in_ref[:, O_K:O_V]).astype(jnp.bfloat16)
    v_buf[KV_CARRY:KV_CARRY + ts, :] = _dot(h_buf[...], w_in_ref[:, O_V:O_QM]).astype(jnp.bfloat16)
    split_heads(_dot(h_buf[...], w_in_ref[:, O_QM:O_GATE]), qm_buf.at[0], qm_buf.at[1])

    u = u_buf[POOL_TAIL:POOL_TAIL + ts, :]
    lane_group = lax.broadcasted_iota(jnp.int32, (ts, POOL_WIDTH), 1) // POOL_GROUP_DIM
    run = u
    win_sum = jnp.zeros_like(u)
    for j in range(1, POOL_TAIL + 1):
        if j in POOL_WINDOWS:
            win_sum = jnp.where(lane_group == POOL_WINDOWS.index(j), run, win_sum)
        if j < POOL_TAIL:
            run = run + u_buf[POOL_TAIL - j:POOL_TAIL - j + ts, :]
    window = jnp.left_shift(2, lane_group)
    pos1 = tile * ts + lax.broadcasted_iota(jnp.int32, (ts, POOL_WIDTH), 0) + 1
    cnt = jnp.minimum(pos1, window).astype(jnp.float32)
    pooled = (win_sum / cnt - u).astype(jnp.bfloat16)
    pool_buf[...] = (_dot(pooled, w_pool_ref[...]) * pool_scale_ref[...]).astype(jnp.bfloat16)
    u_buf[0:POOL_TAIL, :] = u_buf[ts:ts + POOL_TAIL, :]

    lane_is_odd_head = lax.broadcasted_iota(jnp.int32, (CHUNK, PAIR), 1) >= HEAD_DIM
    win_col = lax.broadcasted_iota(jnp.int32, (1, WIN_KEYS), 1)
    pad_rows = jnp.where(tile == 0, KV_CARRY, 0)

    def chunk_body(c, carry):
        r0 = pl.multiple_of(c * CHUNK, CHUNK)
        pen = jnp.where(win_col + r0 < pad_rows, NEG_INF, 0.0)
        for p in range(ATTN_HEADS // 2):
            cols = slice(p * PAIR, (p + 1) * PAIR)
            k_win = k_buf[pl.ds(r0, WIN_KEYS), cols]
            v_win = v_buf[pl.ds(r0, WIN_KEYS), cols]
            outs = []
            for e in range(2):
                s = _dot_nt(q_buf[e, pl.ds(r0, CHUNK), cols], k_win) + bias_ref[2 * p + e] + pen
                outs.append(_softmax_pv(s, v_win))
            attn_buf[pl.ds(r0, CHUNK), cols] = jnp.where(lane_is_odd_head, outs[1], outs[0]).astype(jnp.bfloat16)
        return carry

    lax.fori_loop(0, ts // CHUNK, chunk_body, 0)
    k_buf[CHUNK:KV_CARRY, :] = k_buf[ts + CHUNK:ts + KV_CARRY, :]
    v_buf[CHUNK:KV_CARRY, :] = v_buf[ts + CHUNK:ts + KV_CARRY, :]

    mem_lane_is_odd = lax.broadcasted_iota(jnp.int32, (MEM_ROWS, PAIR), 1) >= HEAD_DIM

    def mem_body(r, carry):
        r0 = pl.multiple_of(r * MEM_ROWS, MEM_ROWS)
        for p in range(MEM_HEADS // 2):
            cols = slice(p * PAIR, (p + 1) * PAIR)
            outs = []
            for e in range(2):
                s = _dot_nt(qm_buf[e, pl.ds(r0, MEM_ROWS), cols], km_ref[:, cols])
                outs.append(_softmax_pv(s, vm_ref[:, cols]))
            mem_buf[pl.ds(r0, MEM_ROWS), cols] = jnp.where(mem_lane_is_odd, outs[1], outs[0]).astype(jnp.bfloat16)
        return carry

    lax.fori_loop(0, ts // MEM_ROWS, mem_body, 0)

    branches = ((pool_buf, w_up_pool_ref), (attn_buf, w_up_attn_ref), (mem_buf, w_up_mem_ref))
    blk = 2 * LANES
    for nb in range(D_MODEL // blk):
        cols = slice(nb * blk, (nb + 1) * blk)
        merged = jnp.zeros((ts, blk), jnp.float32)
        for b, (branch_buf, w_up_ref) in enumerate(branches):
            gcols = slice(O_GATE + b * D_MODEL + nb * blk, O_GATE + b * D_MODEL + (nb + 1) * blk)
            logits = _dot(h_buf[...], w_in_ref[:, gcols]) + b_gate_ref[:, b * D_MODEL + nb * blk:b * D_MODEL + (nb + 1) * blk]
            merged = merged + jax.nn.sigmoid(logits) * _dot(branch_buf[...], w_up_ref[:, cols])
        merged_buf[:, cols] = merged.astype(jnp.bfloat16)
    o_ref[...] = x_ref[...] + _dot(merged_buf[...], w_out_ref[...])


def _ffn_kernel(x_ref, g_ref, w_up_ref, conv_w_ref, conv_b_ref, w_down_ref, g_final_ref, o_ref,
                h_buf, a_buf, tail_buf, hid_buf):
    ts = x_ref.shape[0]

    @pl.when(pl.program_id(0) == 0)
    def _init():
        tail_buf[...] = jnp.zeros_like(tail_buf)

    h_buf[...] = _rms_norm(x_ref[...], g_ref[...]).astype(jnp.bfloat16)

    def conv_up(cols):
        a = _dot(h_buf[...], w_up_ref[:, cols])
        a_buf[0:CONV_PAD, :] = tail_buf[:, cols]
        a_buf[CONV_PAD:CONV_PAD + ts, :] = a
        tail_buf[:, cols] = a_buf[ts:ts + CONV_PAD, :]
        out = conv_b_ref[:, cols] + a * conv_w_ref[CONV_WIDTH - 1:CONV_WIDTH, cols]
        for t in range(CONV_WIDTH - 1):
            back = CONV_WIDTH - 1 - t
            out = out + a_buf[CONV_PAD - back:CONV_PAD - back + ts, :] * conv_w_ref[t:t + 1, cols]
        return out

    for jb in range(D_FF // FF_BLOCK):
        gate = conv_up(slice(jb * FF_BLOCK, (jb + 1) * FF_BLOCK))
        val = conv_up(slice(D_FF + jb * FF_BLOCK, D_FF + (jb + 1) * FF_BLOCK))
        act = 0.5 * gate * (1.0 + lax.erf(gate * (2.0 ** -0.5)))
        hid_buf[:, jb * FF_BLOCK:(jb + 1) * FF_BLOCK] = (act * val).astype(jnp.bfloat16)

    y = x_ref[...] + _dot(hid_buf[...], w_down_ref[...])
    o_ref[...] = _rms_norm(y, g_final_ref[...])


def _resident(shape):
    zeros = (0,) * len(shape)
    return pl.BlockSpec(shape, lambda i: zeros, pipeline_mode=pl.Buffered(1))


def _window_bias(rel_bias):
    qi = jnp.arange(CHUNK)
    kj = jnp.arange(BAND_KEYS)
    rel = qi[:, None] + (BAND_CHUNKS - 1) * CHUNK - kj[None, :]
    idx = jnp.clip(rel, -MAX_REL, MAX_REL) + MAX_REL
    band = rel_bias[:, idx].astype(jnp.float32)
    pad = jnp.full((ATTN_HEADS, CHUNK, WIN_KEYS - BAND_KEYS), NEG_INF, jnp.float32)
    return jnp.concatenate([pad, band], axis=-1)


def _block_diag(w):
    g, c, _ = w.shape
    eye = jnp.eye(g, dtype=w.dtype)
    return (eye[:, None, :, None] * w[:, :, None, :]).reshape(g * c, g * c)


def kernel(x, mem, norm_mix_g, norm_mem_g, w_in, b_gate, w_pool, pool_scale, rel_bias, w_mem_kv,
           w_up_pool, w_up_attn, w_up_mem, w_out, norm_ffn_g, w_ffn_up, conv_w, conv_b, w_ffn_down,
           norm_final_g):
    batch, seq, d = x.shape
    assert batch == 1 and d == D_MODEL and seq % SEQ_TILE == 0
    assert norm_mix_g.shape[0] == 1, "single layer"
    bf16 = jnp.bfloat16
    f32 = jnp.float32
    ts = SEQ_TILE
    n_tiles = seq // ts
    mem_len = mem.shape[1]
    row = lambda a: a.reshape(1, -1).astype(f32)
    params = pltpu.CompilerParams(dimension_semantics=("arbitrary",), vmem_limit_bytes=VMEM_LIMIT)

    km, vm = pl.pallas_call(
        _mem_kv_kernel,
        out_shape=(jax.ShapeDtypeStruct((mem_len, MEM_WIDTH), bf16),) * 2,
        name="mem_kv",
    )(mem[0], row(norm_mem_g[0]), w_mem_kv[0].astype(bf16))

    tile_spec = pl.BlockSpec((ts, D_MODEL), lambda i: (i, 0))
    mixer_inputs = (
        row(norm_mix_g[0]), w_in[0].astype(bf16), row(b_gate[0]), _block_diag(w_pool[0]).astype(bf16),
        row(pool_scale[0]), _window_bias(rel_bias[0]), km, vm,
        w_up_pool[0].astype(bf16), w_up_attn[0].astype(bf16), w_up_mem[0].astype(bf16), w_out[0].astype(bf16),
    )
    x1 = pl.pallas_call(
        _mixer_kernel,
        grid=(n_tiles,),
        in_specs=[tile_spec] + [_resident(a.shape) for a in mixer_inputs],
        out_specs=tile_spec,
        out_shape=jax.ShapeDtypeStruct((seq, D_MODEL), f32),
        scratch_shapes=[
            pltpu.VMEM((ts, D_MODEL), bf16),
            pltpu.VMEM((POOL_TAIL + ts, POOL_WIDTH), f32),
            pltpu.VMEM((2, ts, ATTN_WIDTH), bf16),
            pltpu.VMEM((KV_CARRY + ts, ATTN_WIDTH), bf16),
            pltpu.VMEM((KV_CARRY + ts, ATTN_WIDTH), bf16),
            pltpu.VMEM((2, ts, MEM_WIDTH), bf16),
            pltpu.VMEM((ts, POOL_WIDTH), bf16),
            pltpu.VMEM((ts, ATTN_WIDTH), bf16),
            pltpu.VMEM((ts, MEM_WIDTH), bf16),
            pltpu.VMEM((ts, D_MODEL), bf16),
        ],
        compiler_params=params,
        name="mixer",
    )(x[0], *mixer_inputs)

    ffn_inputs = (
        row(norm_ffn_g[0]), w_ffn_up[0].astype(bf16), conv_w[0].astype(f32), row(conv_b[0]),
        w_ffn_down[0].astype(bf16), row(norm_final_g),
    )
    out = pl.pallas_call(
        _ffn_kernel,
        grid=(n_tiles,),
        in_specs=[tile_spec] + [_resident(a.shape) for a in ffn_inputs],
        out_specs=tile_spec,
        out_shape=jax.ShapeDtypeStruct((seq, D_MODEL), x.dtype),
        scratch_shapes=[
            pltpu.VMEM((ts, D_MODEL), bf16),
            pltpu.VMEM((CONV_PAD + ts, FF_BLOCK), f32),
            pltpu.VMEM((CONV_PAD, 2 * D_FF), f32),
            pltpu.VMEM((ts, D_FF), bf16),
        ],
        compiler_params=params,
        name="ffn",
    )(x1, *ffn_inputs)
    return out[None]
```

```python
import functools

import jax
import jax.numpy as jnp
from jax import lax
from jax.experimental import pallas as pl
from jax.experimental.pallas import tpu as pltpu

D_MODEL = 1024
CHUNK = 64
HEAD_DIM = 64
POOL_WINDOWS = (2, 4, 8, 16)
POOL_GROUP_DIM = 64
POOL_WIDTH = len(POOL_WINDOWS) * POOL_GROUP_DIM
ATTN_HEADS = 8
ATTN_WIDTH = ATTN_HEADS * HEAD_DIM
BAND_CHUNKS = 9
BAND_KEYS = BAND_CHUNKS * CHUNK
MAX_REL = 128
MEM_HEADS = 4
MEM_WIDTH = MEM_HEADS * HEAD_DIM
N_BRANCH = 3
D_FF = 2816
CONV_WIDTH = 3
RMS_EPS = 1e-6
NEG_INF = -1e30

O_POOL = 0
O_Q = O_POOL + POOL_WIDTH
O_K = O_Q + ATTN_WIDTH
O_V = O_K + ATTN_WIDTH
O_QM = O_V + ATTN_WIDTH
O_GATE = O_QM + MEM_WIDTH

LANES = 128
SUBLANES = 8
PAIR = 2 * HEAD_DIM
POOL_TAIL = 16
WIN_KEYS = BAND_KEYS + CHUNK
KV_CARRY = WIN_KEYS - CHUNK
SEQ_TILE = 512
MEM_ROWS = 128
FF_BLOCK = 256
CONV_PAD = SUBLANES
VMEM_LIMIT = 56 * 1024 * 1024


def _rms_norm(x, g):
    y = x * lax.rsqrt(jnp.mean(x * x, axis=-1, keepdims=True) + RMS_EPS)
    return y * g


def _dot(a, b):
    return jnp.dot(a, b, preferred_element_type=jnp.float32)


def _dot_nt(a, b):
    return lax.dot_general(a, b, (((1,), (1,)), ((), ())), preferred_element_type=jnp.float32)


def _softmax_pv(s, v):
    m = jnp.max(s, axis=-1, keepdims=True)
    e = jnp.exp(s - m)
    l = jnp.sum(e, axis=-1, keepdims=True)
    return _dot(e.astype(jnp.bfloat16), v) / l


def _mem_kv_kernel(mem_ref, g_ref, w_ref, k_ref, v_ref):
    mem_n = _rms_norm(mem_ref[...], g_ref[...]).astype(jnp.bfloat16)
    kv = _dot(mem_n, w_ref[...])
    k_ref[...] = kv[:, :MEM_WIDTH].astype(jnp.bfloat16)
    v_ref[...] = kv[:, MEM_WIDTH:].astype(jnp.bfloat16)


def _mixer_kernel(x_ref, g_ref, w_in_ref, b_gate_ref, w_pool_ref, pool_scale_ref, rel_vec_ref,
                  km_ref, vm_ref, w_up_pool_ref, w_up_attn_ref, w_up_mem_ref, w_out_ref,
                  o_ref,
                  h_buf, u_buf, q_buf, k_buf, v_buf, qm_buf, pool_buf, attn_buf, mem_buf, merged_buf,
                  bias_buf, s_buf, p_buf):
    ts = x_ref.shape[0]
    tile = pl.program_id(0)

    @pl.when(tile == 0)
    def _init():
        u_buf[0:POOL_TAIL, :] = jnp.zeros((POOL_TAIL, POOL_WIDTH), jnp.float32)
        k_buf[0:KV_CARRY, :] = jnp.zeros((KV_CARRY, ATTN_WIDTH), jnp.bfloat16)
        v_buf[0:KV_CARRY, :] = jnp.zeros((KV_CARRY, ATTN_WIDTH), jnp.bfloat16)
        masked = lax.broadcasted_iota(jnp.int32, (1, WIN_KEYS), 1) < WIN_KEYS - BAND_KEYS
        for h in range(ATTN_HEADS):
            vec = rel_vec_ref[h:h + 1, :]
            for qi in range(CHUNK):
                row = pltpu.roll(vec, qi, axis=1) if qi else vec
                bias_buf[h, qi:qi + 1, :] = jnp.where(masked, NEG_INF, row)

    h_buf[...] = _rms_norm(x_ref[...], g_ref[...]).astype(jnp.bfloat16)

    u_buf[POOL_TAIL:POOL_TAIL + ts, :] = _dot(h_buf[...], w_in_ref[:, O_POOL:O_Q])
    def split_heads(q, even_ref, odd_ref):
        odd = (lax.broadcasted_iota(jnp.int32, q.shape, 1) // HEAD_DIM) % 2 == 1
        q = q * (HEAD_DIM ** -0.5)
        even_ref[...] = jnp.where(odd, 0.0, q).astype(jnp.bfloat16)
        odd_ref[...] = jnp.where(odd, q, 0.0).astype(jnp.bfloat16)

    split_heads(_dot(h_buf[...], w_in_ref[:, O_Q:O_K]), q_buf.at[0], q_buf.at[1])
    k_buf[KV_CARRY:KV_CARRY + ts, :] = _dot(h_buf[...], w_in_ref[:, O_K:O_V]).astype(jnp.bfloat16)
    v_buf[KV_CARRY:KV_CARRY + ts, :] = _dot(h_buf[...], w_in_ref[:, O_V:O_QM]).astype(jnp.bfloat16)
    split_heads(_dot(h_buf[...], w_in_ref[:, O_QM:O_GATE]), qm_buf.at[0], qm_buf.at[1])

    u = u_buf[POOL_TAIL:POOL_TAIL + ts, :]
    lane_group = lax.broadcasted_iota(jnp.int32, (ts, POOL_WIDTH), 1) // POOL_GROUP_DIM
    run = u
    win_sum = jnp.zeros_like(u)
    for j in range(1, POOL_TAIL + 1):
        if j in POOL_WINDOWS:
            win_sum = jnp.where(lane_group == POOL_WINDOWS.index(j), run, win_sum)
        if j < POOL_TAIL:
            run = run + u_buf[POOL_TAIL - j:POOL_TAIL - j + ts, :]
    window = jnp.left_shift(2, lane_group)
    pos1 = tile * ts + lax.broadcasted_iota(jnp.int32, (ts, POOL_WIDTH), 0) + 1
    cnt = jnp.minimum(pos1, window).astype(jnp.float32)
    pooled = (win_sum / cnt - u).astype(jnp.bfloat16)
    pool_buf[...] = (_dot(pooled, w_pool_ref[...]) * pool_scale_ref[...]).astype(jnp.bfloat16)
    u_buf[0:POOL_TAIL, :] = u_buf[ts:ts + POOL_TAIL, :]

    lane_is_odd_head = lax.broadcasted_iota(jnp.int32, (CHUNK, PAIR), 1) >= HEAD_DIM
    win_col = lax.broadcasted_iota(jnp.int32, (1, WIN_KEYS), 1)
    pad_rows = jnp.where(tile == 0, KV_CARRY, 0)

    def pair_cols(h):
        return slice((h // 2) * PAIR, (h // 2 + 1) * PAIR)

    def chunk_body(c, carry):
        r0 = pl.multiple_of(c * CHUNK, CHUNK)
        pen = jnp.where(win_col + r0 < pad_rows, NEG_INF, 0.0)
        for h in range(ATTN_HEADS):
            s = _dot_nt(q_buf[h % 2, pl.ds(r0, CHUNK), pair_cols(h)], k_buf[pl.ds(r0, WIN_KEYS), pair_cols(h)])
            s_buf[h] = s + bias_buf[h] + pen
        row_sums = []
        for h in range(ATTN_HEADS):
            s = s_buf[h]
            e = jnp.exp(s - jnp.max(s, axis=-1, keepdims=True))
            row_sums.append(jnp.sum(e, axis=-1, keepdims=True))
            p_buf[h] = e.astype(jnp.bfloat16)
        outs = [_dot(p_buf[h], v_buf[pl.ds(r0, WIN_KEYS), pair_cols(h)]) / row_sums[h] for h in range(ATTN_HEADS)]
        for h in range(0, ATTN_HEADS, 2):
            attn_buf[pl.ds(r0, CHUNK), pair_cols(h)] = jnp.where(lane_is_odd_head, outs[h + 1], outs[h]).astype(jnp.bfloat16)
        return carry

    lax.fori_loop(0, ts // CHUNK, chunk_body, 0)
    k_buf[CHUNK:KV_CARRY, :] = k_buf[ts + CHUNK:ts + KV_CARRY, :]
    v_buf[CHUNK:KV_CARRY, :] = v_buf[ts + CHUNK:ts + KV_CARRY, :]

    mem_lane_is_odd = lax.broadcasted_iota(jnp.int32, (MEM_ROWS, PAIR), 1) >= HEAD_DIM

    def mem_body(r, carry):
        r0 = pl.multiple_of(r * MEM_ROWS, MEM_ROWS)
        scores = [_dot_nt(qm_buf[h % 2, pl.ds(r0, MEM_ROWS), pair_cols(h)], km_ref[:, pair_cols(h)])
                  for h in range(MEM_HEADS)]
        outs = [_softmax_pv(scores[h], vm_ref[:, pair_cols(h)]) for h in range(MEM_HEADS)]
        for h in range(0, MEM_HEADS, 2):
            mem_buf[pl.ds(r0, MEM_ROWS), pair_cols(h)] = jnp.where(mem_lane_is_odd, outs[h + 1], outs[h]).astype(jnp.bfloat16)
        return carry

    lax.fori_loop(0, ts // MEM_ROWS, mem_body, 0)

    branches = ((pool_buf, w_up_pool_ref), (attn_buf, w_up_attn_ref), (mem_buf, w_up_mem_ref))
    blk = 2 * LANES
    for nb in range(D_MODEL // blk):
        cols = slice(nb * blk, (nb + 1) * blk)
        merged = jnp.zeros((ts, blk), jnp.float32)
        for b, (branch_buf, w_up_ref) in enumerate(branches):
            gcols = slice(O_GATE + b * D_MODEL + nb * blk, O_GATE + b * D_MODEL + (nb + 1) * blk)
            logits = _dot(h_buf[...], w_in_ref[:, gcols]) + b_gate_ref[:, b * D_MODEL + nb * blk:b * D_MODEL + (nb + 1) * blk]
            merged = merged + jax.nn.sigmoid(logits) * _dot(branch_buf[...], w_up_ref[:, cols])
        merged_buf[:, cols] = merged.astype(jnp.bfloat16)
    o_ref[...] = x_ref[...] + _dot(merged_buf[...], w_out_ref[...])


def _ffn_kernel(x_ref, g_ref, w_up_ref, conv_w_ref, conv_b_ref, w_down_ref, g_final_ref, o_ref,
                h_buf, a_buf, tail_buf, hid_buf):
    ts = x_ref.shape[0]

    @pl.when(pl.program_id(0) == 0)
    def _init():
        tail_buf[...] = jnp.zeros_like(tail_buf)

    h_buf[...] = _rms_norm(x_ref[...], g_ref[...]).astype(jnp.bfloat16)

    def conv_up(cols):
        a = _dot(h_buf[...], w_up_ref[:, cols])
        a_buf[0:CONV_PAD, :] = tail_buf[:, cols]
        a_buf[CONV_PAD:CONV_PAD + ts, :] = a
        tail_buf[:, cols] = a_buf[ts:ts + CONV_PAD, :]
        out = conv_b_ref[:, cols] + a * conv_w_ref[CONV_WIDTH - 1:CONV_WIDTH, cols]
        for t in range(CONV_WIDTH - 1):
            back = CONV_WIDTH - 1 - t
            out = out + a_buf[CONV_PAD - back:CONV_PAD - back + ts, :] * conv_w_ref[t:t + 1, cols]
        return out

    for jb in range(D_FF // FF_BLOCK):
        gate = conv_up(slice(jb * FF_BLOCK, (jb + 1) * FF_BLOCK))
        val = conv_up(slice(D_FF + jb * FF_BLOCK, D_FF + (jb + 1) * FF_BLOCK))
        act = 0.5 * gate * (1.0 + lax.erf(gate * (2.0 ** -0.5)))
        hid_buf[:, jb * FF_BLOCK:(jb + 1) * FF_BLOCK] = (act * val).astype(jnp.bfloat16)

    y = x_ref[...] + _dot(hid_buf[...], w_down_ref[...])
    o_ref[...] = _rms_norm(y, g_final_ref[...])


def _resident(shape):
    zeros = (0,) * len(shape)
    return pl.BlockSpec(shape, lambda i: zeros, pipeline_mode=pl.Buffered(1))


def _rel_vector(rel_bias):
    assert CHUNK - 1 <= MAX_REL <= WIN_KEYS - CHUNK - 1
    n_clipped = WIN_KEYS - CHUNK - MAX_REL
    far = jnp.broadcast_to(rel_bias[:, 2 * MAX_REL:], (ATTN_HEADS, n_clipped))
    near = rel_bias[:, MAX_REL - (CHUNK - 1):2 * MAX_REL][:, ::-1]
    unused = jnp.zeros((ATTN_HEADS, 1), rel_bias.dtype)
    return jnp.concatenate([unused, far, near], axis=1).astype(jnp.float32)


def _block_diag(w):
    g, c, _ = w.shape
    eye = jnp.eye(g, dtype=w.dtype)
    return (eye[:, None, :, None] * w[:, :, None, :]).reshape(g * c, g * c)


def kernel(x, mem, norm_mix_g, norm_mem_g, w_in, b_gate, w_pool, pool_scale, rel_bias, w_mem_kv,
           w_up_pool, w_up_attn, w_up_mem, w_out, norm_ffn_g, w_ffn_up, conv_w, conv_b, w_ffn_down,
           norm_final_g):
    batch, seq, d = x.shape
    assert batch == 1 and d == D_MODEL and seq % SEQ_TILE == 0
    assert norm_mix_g.shape[0] == 1, "single layer"
    bf16 = jnp.bfloat16
    f32 = jnp.float32
    ts = SEQ_TILE
    n_tiles = seq // ts
    mem_len = mem.shape[1]
    row = lambda a: a.reshape(1, -1).astype(f32)
    params = pltpu.CompilerParams(dimension_semantics=("arbitrary",), vmem_limit_bytes=VMEM_LIMIT)

    km, vm = pl.pallas_call(
        _mem_kv_kernel,
        out_shape=(jax.ShapeDtypeStruct((mem_len, MEM_WIDTH), bf16),) * 2,
        name="mem_kv",
    )(mem[0], row(norm_mem_g[0]), w_mem_kv[0].astype(bf16))

    tile_spec = pl.BlockSpec((ts, D_MODEL), lambda i: (i, 0))
    mixer_inputs = (
        row(norm_mix_g[0]), w_in[0].astype(bf16), row(b_gate[0]), _block_diag(w_pool[0]).astype(bf16),
        row(pool_scale[0]), _rel_vector(rel_bias[0]), km, vm,
        w_up_pool[0].astype(bf16), w_up_attn[0].astype(bf16), w_up_mem[0].astype(bf16), w_out[0].astype(bf16),
    )
    x1 = pl.pallas_call(
        _mixer_kernel,
        grid=(n_tiles,),
        in_specs=[tile_spec] + [_resident(a.shape) for a in mixer_inputs],
        out_specs=tile_spec,
        out_shape=jax.ShapeDtypeStruct((seq, D_MODEL), f32),
        scratch_shapes=[
            pltpu.VMEM((ts, D_MODEL), bf16),
            pltpu.VMEM((POOL_TAIL + ts, POOL_WIDTH), f32),
            pltpu.VMEM((2, ts, ATTN_WIDTH), bf16),
            pltpu.VMEM((KV_CARRY + ts, ATTN_WIDTH), bf16),
            pltpu.VMEM((KV_CARRY + ts, ATTN_WIDTH), bf16),
            pltpu.VMEM((2, ts, MEM_WIDTH), bf16),
            pltpu.VMEM((ts, POOL_WIDTH), bf16),
            pltpu.VMEM((ts, ATTN_WIDTH), bf16),
            pltpu.VMEM((ts, MEM_WIDTH), bf16),
            pltpu.VMEM((ts, D_MODEL), bf16),
            pltpu.VMEM((ATTN_HEADS, CHUNK, WIN_KEYS), f32),
            pltpu.VMEM((ATTN_HEADS, CHUNK, WIN_KEYS), f32),
            pltpu.VMEM((ATTN_HEADS, CHUNK, WIN_KEYS), bf16),
        ],
        compiler_params=params,
        name="mixer",
    )(x[0], *mixer_inputs)

    ffn_inputs = (
        row(norm_ffn_g[0]), w_ffn_up[0].astype(bf16), conv_w[0].astype(f32), row(conv_b[0]),
        w_ffn_down[0].astype(bf16), row(norm_final_g),
    )
    out = pl.pallas_call(
        _ffn_kernel,
        grid=(n_tiles,),
        in_specs=[tile_spec] + [_resident(a.shape) for a in ffn_inputs],
        out_specs=tile_spec,
        out_shape=jax.ShapeDtypeStruct((seq, D_MODEL), x.dtype),
        scratch_shapes=[
            pltpu.VMEM((ts, D_MODEL), bf16),
            pltpu.VMEM((CONV_PAD + ts, FF_BLOCK), f32),
            pltpu.VMEM((CONV_PAD, 2 * D_FF), f32),
            pltpu.VMEM((ts, D_FF), bf16),
        ],
        compiler_params=params,
        name="ffn",
    )(x1, *ffn_inputs)
    return out[None]
```

```python
import functools

import jax
import jax.numpy as jnp
from jax import lax
from jax.experimental import pallas as pl
from jax.experimental.pallas import tpu as pltpu

D_MODEL = 1024
CHUNK = 64
HEAD_DIM = 64
POOL_WINDOWS = (2, 4, 8, 16)
POOL_GROUP_DIM = 64
POOL_WIDTH = len(POOL_WINDOWS) * POOL_GROUP_DIM
ATTN_HEADS = 8
ATTN_WIDTH = ATTN_HEADS * HEAD_DIM
BAND_CHUNKS = 9
BAND_KEYS = BAND_CHUNKS * CHUNK
MAX_REL = 128
MEM_HEADS = 4
MEM_WIDTH = MEM_HEADS * HEAD_DIM
N_BRANCH = 3
D_FF = 2816
CONV_WIDTH = 3
RMS_EPS = 1e-6
NEG_INF = -1e30

O_POOL = 0
O_Q = O_POOL + POOL_WIDTH
O_K = O_Q + ATTN_WIDTH
O_V = O_K + ATTN_WIDTH
O_QM = O_V + ATTN_WIDTH
O_GATE = O_QM + MEM_WIDTH

LANES = 128
SUBLANES = 8
PAIR = 2 * HEAD_DIM
POOL_TAIL = 16
WIN_KEYS = BAND_KEYS + CHUNK
KV_CARRY = WIN_KEYS - CHUNK
UNION_KEYS = WIN_KEYS + 2 * CHUNK
KV_TAIL = CHUNK
SEQ_TILE = 512
MEM_ROWS = 128
FF_BLOCK = 256
CONV_PAD = SUBLANES
VMEM_LIMIT = 56 * 1024 * 1024


def _rms_norm(x, g):
    y = x * lax.rsqrt(jnp.mean(x * x, axis=-1, keepdims=True) + RMS_EPS)
    return y * g


def _dot(a, b):
    return jnp.dot(a, b, preferred_element_type=jnp.float32)


def _dot_nt(a, b):
    return lax.dot_general(a, b, (((1,), (1,)), ((), ())), preferred_element_type=jnp.float32)


def _softmax_pv(s, v):
    m = jnp.max(s, axis=-1, keepdims=True)
    e = jnp.exp(s - m)
    l = jnp.sum(e, axis=-1, keepdims=True)
    return _dot(e.astype(jnp.bfloat16), v) / l


def _mem_kv_kernel(mem_ref, g_ref, w_ref, k_ref, v_ref):
    mem_n = _rms_norm(mem_ref[...], g_ref[...]).astype(jnp.bfloat16)
    kv = _dot(mem_n, w_ref[...])
    k_ref[...] = kv[:, :MEM_WIDTH].astype(jnp.bfloat16)
    v_ref[...] = kv[:, MEM_WIDTH:].astype(jnp.bfloat16)


def _mixer_kernel(zero_ref, x_ref, g_ref, w_in_ref, b_gate_ref, w_pool_ref, pool_scale_ref, rel_vec_ref,
                  km_ref, vm_ref, w_up_pool_ref, w_up_attn_ref, w_up_mem_ref, w_out_ref,
                  o_ref,
                  h_buf, u_buf, q_buf, k_buf, v_buf, qm_buf, pool_buf, attn_buf, mem_buf, merged_buf,
                  bias_buf, s_buf0, s_buf1, m_buf0, m_buf1, p_buf0, p_buf1):
    s_buf, m_buf, p_buf = (s_buf0, s_buf1), (m_buf0, m_buf1), (p_buf0, p_buf1)
    ts = x_ref.shape[0]
    tile = pl.program_id(0)

    @pl.when(tile == 0)
    def _init():
        u_buf[0:POOL_TAIL, :] = jnp.zeros((POOL_TAIL, POOL_WIDTH), jnp.float32)
        k_buf[...] = jnp.zeros(k_buf.shape, jnp.bfloat16)
        ext_lane = lax.broadcasted_iota(jnp.int32, v_buf.shape[1:], 1) % (2 * PAIR)
        for e in range(2):
            ones = jnp.logical_and(ext_lane >= PAIR + e * HEAD_DIM, ext_lane < PAIR + (e + 1) * HEAD_DIM)
            v_buf[e] = jnp.where(ones, 1.0, 0.0).astype(jnp.bfloat16)
        for buf in p_buf:
            buf[...] = jnp.zeros(buf.shape, jnp.bfloat16)
        win_lane = lax.broadcasted_iota(jnp.int32, (1, WIN_KEYS), 1)
        for h in range(ATTN_HEADS):
            vec = rel_vec_ref[h:h + 1, :]
            for qi in range(CHUNK):
                r = (h % 2) * CHUNK + qi
                even_row = pltpu.roll(vec, qi, axis=1) if qi else vec
                bias_buf[0, h // 2, r:r + 1, :] = jnp.where(win_lane < CHUNK, NEG_INF, even_row)
                odd_row = pltpu.roll(vec, qi + BAND_KEYS, axis=1)
                bias_buf[1, h // 2, r:r + 1, :] = jnp.where(win_lane >= BAND_KEYS, NEG_INF, odd_row)

    h_buf[...] = _rms_norm(x_ref[...], g_ref[...]).astype(jnp.bfloat16)

    u_buf[POOL_TAIL:POOL_TAIL + ts, :] = _dot(h_buf[...], w_in_ref[:, O_POOL:O_Q])
    def split_heads(q, even_ref, odd_ref):
        odd = (lax.broadcasted_iota(jnp.int32, q.shape, 1) // HEAD_DIM) % 2 == 1
        q = q * (HEAD_DIM ** -0.5)
        even_ref[...] = jnp.where(odd, 0.0, q).astype(jnp.bfloat16)
        odd_ref[...] = jnp.where(odd, q, 0.0).astype(jnp.bfloat16)

    split_heads(_dot(h_buf[...], w_in_ref[:, O_Q:O_K]), q_buf.at[0], q_buf.at[1])
    k_buf[KV_CARRY:KV_CARRY + ts, :] = _dot(h_buf[...], w_in_ref[:, O_K:O_V]).astype(jnp.bfloat16)
    v = _dot(h_buf[...], w_in_ref[:, O_V:O_QM])
    pair_lane_is_odd = lax.broadcasted_iota(jnp.int32, (ts, PAIR), 1) >= HEAD_DIM
    for p in range(ATTN_HEADS // 2):
        v_pair = v[:, p * PAIR:(p + 1) * PAIR]
        vcols = slice(p * 2 * PAIR, p * 2 * PAIR + PAIR)
        v_buf[0, KV_CARRY:KV_CARRY + ts, vcols] = jnp.where(pair_lane_is_odd, 0.0, v_pair).astype(jnp.bfloat16)
        v_buf[1, KV_CARRY:KV_CARRY + ts, vcols] = jnp.where(pair_lane_is_odd, v_pair, 0.0).astype(jnp.bfloat16)
    split_heads(_dot(h_buf[...], w_in_ref[:, O_QM:O_GATE]), qm_buf.at[0], qm_buf.at[1])

    u = u_buf[POOL_TAIL:POOL_TAIL + ts, :]
    lane_group = lax.broadcasted_iota(jnp.int32, (ts, POOL_WIDTH), 1) // POOL_GROUP_DIM
    run = u
    win_sum = jnp.zeros_like(u)
    for j in range(1, POOL_TAIL + 1):
        if j in POOL_WINDOWS:
            win_sum = jnp.where(lane_group == POOL_WINDOWS.index(j), run, win_sum)
        if j < POOL_TAIL:
            run = run + u_buf[POOL_TAIL - j:POOL_TAIL - j + ts, :]
    window = jnp.left_shift(2, lane_group)
    pos1 = tile * ts + lax.broadcasted_iota(jnp.int32, (ts, POOL_WIDTH), 0) + 1
    cnt = jnp.minimum(pos1, window).astype(jnp.float32)
    pooled = (win_sum / cnt - u).astype(jnp.bfloat16)
    pool_buf[...] = (_dot(pooled, w_pool_ref[...]) * pool_scale_ref[...]).astype(jnp.bfloat16)
    u_buf[0:POOL_TAIL, :] = u_buf[ts:ts + POOL_TAIL, :]

    win_col = lax.broadcasted_iota(jnp.int32, (1, WIN_KEYS), 1)
    pad_rows = jnp.where(tile == 0, KV_CARRY, 0)

    def pair_cols(h):
        return slice((h // 2) * PAIR, (h // 2 + 1) * PAIR)

    def ext_cols(p):
        return slice(p * 2 * PAIR, (p + 1) * 2 * PAIR)

    n_k = WIN_KEYS // LANES
    n_cp = ts // (2 * CHUNK)
    staged = zero_ref[0]

    def score_phase(cp):
        base = cp * 2 * CHUNK
        slot = cp % 2
        for j in range(2):
            q0 = base + j * CHUNK
            w0 = base + j * 2 * CHUNK
            pen = jnp.where(win_col + w0 < pad_rows, NEG_INF, 0.0)
            for p in range(ATTN_HEADS // 2):
                q_pair = jnp.concatenate([q_buf[0, q0:q0 + CHUNK, pair_cols(2 * p)],
                                          q_buf[1, q0:q0 + CHUNK, pair_cols(2 * p)]], axis=0)
                s = _dot_nt(q_pair, k_buf[w0:w0 + WIN_KEYS, pair_cols(2 * p)]) + bias_buf[j, p] + pen
                s_buf[slot][j, p] = s
                part = s[:, 0:LANES]
                for k in range(1, n_k):
                    part = jnp.maximum(part, s[:, k * LANES:(k + 1) * LANES])
                m_buf[slot][j, p] = jnp.broadcast_to(jnp.max(part, axis=-1, keepdims=True), (2 * CHUNK, LANES))

    def softmax_phase(cp):
        slot = cp % 2
        for j in range(2):
            rows = slice(j * CHUNK, (j + 1) * CHUNK)
            for p in range(ATTN_HEADS // 2):
                m = m_buf[slot][staged + j, p]
                for k in range(n_k):
                    e = jnp.exp(s_buf[slot][staged + j, p, :, k * LANES:(k + 1) * LANES] - m).astype(jnp.bfloat16)
                    lo = j * 2 * CHUNK + k * LANES
                    p_buf[slot][p, rows, lo:lo + LANES] = e[:CHUNK]
                    p_buf[slot][p, rows, UNION_KEYS + lo:UNION_KEYS + lo + LANES] = e[CHUNK:]

    def value_phase(cp):
        base = cp * 2 * CHUNK
        slot = cp % 2
        for p in range(ATTN_HEADS // 2):
            v_pair = jnp.concatenate([v_buf[0, base:base + UNION_KEYS, ext_cols(p)],
                                      v_buf[1, base:base + UNION_KEYS, ext_cols(p)]], axis=0)
            acc = _dot(p_buf[slot][staged + p], v_pair)
            attn_buf[base:base + 2 * CHUNK, pair_cols(2 * p)] = (acc[:, :PAIR] / acc[:, PAIR:]).astype(jnp.bfloat16)

    for step in range(n_cp + 2):
        if step < n_cp:
            score_phase(step)
        if 0 <= step - 1 < n_cp:
            softmax_phase(step - 1)
        if 0 <= step - 2 < n_cp:
            value_phase(step - 2)

    k_buf[CHUNK:KV_CARRY, :] = k_buf[ts + CHUNK:ts + KV_CARRY, :]
    for p in range(ATTN_HEADS // 2):
        vcols = slice(p * 2 * PAIR, p * 2 * PAIR + PAIR)
        v_buf[:, CHUNK:KV_CARRY, vcols] = v_buf[:, ts + CHUNK:ts + KV_CARRY, vcols]

    mem_lane_is_odd = lax.broadcasted_iota(jnp.int32, (MEM_ROWS, PAIR), 1) >= HEAD_DIM

    def mem_body(r, carry):
        r0 = pl.multiple_of(r * MEM_ROWS, MEM_ROWS)
        scores = [_dot_nt(qm_buf[h % 2, pl.ds(r0, MEM_ROWS), pair_cols(h)], km_ref[:, pair_cols(h)])
                  for h in range(MEM_HEADS)]
        outs = [_softmax_pv(scores[h], vm_ref[:, pair_cols(h)]) for h in range(MEM_HEADS)]
        for h in range(0, MEM_HEADS, 2):
            mem_buf[pl.ds(r0, MEM_ROWS), pair_cols(h)] = jnp.where(mem_lane_is_odd, outs[h + 1], outs[h]).astype(jnp.bfloat16)
        return carry

    lax.fori_loop(0, ts // MEM_ROWS, mem_body, 0)

    branches = ((pool_buf, w_up_pool_ref), (attn_buf, w_up_attn_ref), (mem_buf, w_up_mem_ref))
    blk = 2 * LANES
    for nb in range(D_MODEL // blk):
        cols = slice(nb * blk, (nb + 1) * blk)
        merged = jnp.zeros((ts, blk), jnp.float32)
        for b, (branch_buf, w_up_ref) in enumerate(branches):
            gcols = slice(O_GATE + b * D_MODEL + nb * blk, O_GATE + b * D_MODEL + (nb + 1) * blk)
            logits = _dot(h_buf[...], w_in_ref[:, gcols]) + b_gate_ref[:, b * D_MODEL + nb * blk:b * D_MODEL + (nb + 1) * blk]
            merged = merged + jax.nn.sigmoid(logits) * _dot(branch_buf[...], w_up_ref[:, cols])
        merged_buf[:, cols] = merged.astype(jnp.bfloat16)
    o_ref[...] = x_ref[...] + _dot(merged_buf[...], w_out_ref[...])


def _ffn_kernel(x_ref, g_ref, w_up_ref, conv_w_ref, conv_b_ref, w_down_ref, g_final_ref, o_ref,
                h_buf, a_buf, tail_buf, hid_buf):
    ts = x_ref.shape[0]

    @pl.when(pl.program_id(0) == 0)
    def _init():
        tail_buf[...] = jnp.zeros_like(tail_buf)

    h_buf[...] = _rms_norm(x_ref[...], g_ref[...]).astype(jnp.bfloat16)

    def conv_up(cols):
        a = _dot(h_buf[...], w_up_ref[:, cols])
        a_buf[0:CONV_PAD, :] = tail_buf[:, cols]
        a_buf[CONV_PAD:CONV_PAD + ts, :] = a
        tail_buf[:, cols] = a_buf[ts:ts + CONV_PAD, :]
        out = conv_b_ref[:, cols] + a * conv_w_ref[CONV_WIDTH - 1:CONV_WIDTH, cols]
        for t in range(CONV_WIDTH - 1):
            back = CONV_WIDTH - 1 - t
            out = out + a_buf[CONV_PAD - back:CONV_PAD - back + ts, :] * conv_w_ref[t:t + 1, cols]
        return out

    for jb in range(D_FF // FF_BLOCK):
        gate = conv_up(slice(jb * FF_BLOCK, (jb + 1) * FF_BLOCK))
        val = conv_up(slice(D_FF + jb * FF_BLOCK, D_FF + (jb + 1) * FF_BLOCK))
        act = 0.5 * gate * (1.0 + lax.erf(gate * (2.0 ** -0.5)))
        hid_buf[:, jb * FF_BLOCK:(jb + 1) * FF_BLOCK] = (act * val).astype(jnp.bfloat16)

    y = x_ref[...] + _dot(hid_buf[...], w_down_ref[...])
    o_ref[...] = _rms_norm(y, g_final_ref[...])


def _resident(shape):
    zeros = (0,) * len(shape)
    return pl.BlockSpec(shape, lambda i: zeros, pipeline_mode=pl.Buffered(1))


def _rel_vector(rel_bias):
    assert CHUNK - 1 <= MAX_REL <= WIN_KEYS - CHUNK - 1
    n_clipped = WIN_KEYS - CHUNK - MAX_REL
    far = jnp.broadcast_to(rel_bias[:, 2 * MAX_REL:], (ATTN_HEADS, n_clipped))
    near = rel_bias[:, MAX_REL - (CHUNK - 1):2 * MAX_REL][:, ::-1]
    unused = jnp.zeros((ATTN_HEADS, 1), rel_bias.dtype)
    return jnp.concatenate([unused, far, near], axis=1).astype(jnp.float32)


def _block_diag(w):
    g, c, _ = w.shape
    eye = jnp.eye(g, dtype=w.dtype)
    return (eye[:, None, :, None] * w[:, :, None, :]).reshape(g * c, g * c)


def kernel(x, mem, norm_mix_g, norm_mem_g, w_in, b_gate, w_pool, pool_scale, rel_bias, w_mem_kv,
           w_up_pool, w_up_attn, w_up_mem, w_out, norm_ffn_g, w_ffn_up, conv_w, conv_b, w_ffn_down,
           norm_final_g):
    batch, seq, d = x.shape
    assert batch == 1 and d == D_MODEL and seq % SEQ_TILE == 0
    assert norm_mix_g.shape[0] == 1, "single layer"
    bf16 = jnp.bfloat16
    f32 = jnp.float32
    ts = SEQ_TILE
    n_tiles = seq // ts
    mem_len = mem.shape[1]
    row = lambda a: a.reshape(1, -1).astype(f32)
    params = pltpu.CompilerParams(dimension_semantics=("arbitrary",), vmem_limit_bytes=VMEM_LIMIT)

    km, vm = pl.pallas_call(
        _mem_kv_kernel,
        out_shape=(jax.ShapeDtypeStruct((mem_len, MEM_WIDTH), bf16),) * 2,
        name="mem_kv",
    )(mem[0], row(norm_mem_g[0]), w_mem_kv[0].astype(bf16))

    tile_spec = pl.BlockSpec((ts, D_MODEL), lambda i: (i, 0))
    mixer_inputs = (
        row(norm_mix_g[0]), w_in[0].astype(bf16), row(b_gate[0]), _block_diag(w_pool[0]).astype(bf16),
        row(pool_scale[0]), _rel_vector(rel_bias[0]), km, vm,
        w_up_pool[0].astype(bf16), w_up_attn[0].astype(bf16), w_up_mem[0].astype(bf16), w_out[0].astype(bf16),
    )
    x1 = pl.pallas_call(
        _mixer_kernel,
        grid=(n_tiles,),
        in_specs=[pl.BlockSpec(memory_space=pltpu.SMEM), tile_spec] + [_resident(a.shape) for a in mixer_inputs],
        out_specs=tile_spec,
        out_shape=jax.ShapeDtypeStruct((seq, D_MODEL), f32),
        scratch_shapes=[
            pltpu.VMEM((ts, D_MODEL), bf16),
            pltpu.VMEM((POOL_TAIL + ts, POOL_WIDTH), f32),
            pltpu.VMEM((2, ts, ATTN_WIDTH), bf16),
            pltpu.VMEM((KV_CARRY + ts + KV_TAIL, ATTN_WIDTH), bf16),
            pltpu.VMEM((2, KV_CARRY + ts + KV_TAIL, 2 * ATTN_WIDTH), bf16),
            pltpu.VMEM((2, ts, MEM_WIDTH), bf16),
            pltpu.VMEM((ts, POOL_WIDTH), bf16),
            pltpu.VMEM((ts, ATTN_WIDTH), bf16),
            pltpu.VMEM((ts, MEM_WIDTH), bf16),
            pltpu.VMEM((ts, D_MODEL), bf16),
            pltpu.VMEM((2, ATTN_HEADS // 2, 2 * CHUNK, WIN_KEYS), f32),
            pltpu.VMEM((2, ATTN_HEADS // 2, 2 * CHUNK, WIN_KEYS), f32),
            pltpu.VMEM((2, ATTN_HEADS // 2, 2 * CHUNK, WIN_KEYS), f32),
            pltpu.VMEM((2, ATTN_HEADS // 2, 2 * CHUNK, LANES), f32),
            pltpu.VMEM((2, ATTN_HEADS // 2, 2 * CHUNK, LANES), f32),
            pltpu.VMEM((ATTN_HEADS // 2, 2 * CHUNK, 2 * UNION_KEYS), bf16),
            pltpu.VMEM((ATTN_HEADS // 2, 2 * CHUNK, 2 * UNION_KEYS), bf16),
        ],
        compiler_params=params,
        name="mixer",
    )(jnp.zeros((1,), jnp.int32), x[0], *mixer_inputs)

    ffn_inputs = (
        row(norm_ffn_g[0]), w_ffn_up[0].astype(bf16), conv_w[0].astype(f32), row(conv_b[0]),
        w_ffn_down[0].astype(bf16), row(norm_final_g),
    )
    out = pl.pallas_call(
        _ffn_kernel,
        grid=(n_tiles,),
        in_specs=[tile_spec] + [_resident(a.shape) for a in ffn_inputs],
        out_specs=tile_spec,
        out_shape=jax.ShapeDtypeStruct((seq, D_MODEL), x.dtype),
        scratch_shapes=[
            pltpu.VMEM((ts, D_MODEL), bf16),
            pltpu.VMEM((CONV_PAD + ts, FF_BLOCK), f32),
            pltpu.VMEM((CONV_PAD, 2 * D_FF), f32),
            pltpu.VMEM((ts, D_FF), bf16),
        ],
        compiler_params=params,
        name="ffn",
    )(x1, *ffn_inputs)
    return out[None]
```

```python
import functools

import jax
import jax.numpy as jnp
from jax import lax
from jax.experimental import pallas as pl
from jax.experimental.pallas import tpu as pltpu

D_MODEL = 1024
CHUNK = 64
HEAD_DIM = 64
POOL_WINDOWS = (2, 4, 8, 16)
POOL_GROUP_DIM = 64
POOL_WIDTH = len(POOL_WINDOWS) * POOL_GROUP_DIM
ATTN_HEADS = 8
ATTN_WIDTH = ATTN_HEADS * HEAD_DIM
BAND_CHUNKS = 9
BAND_KEYS = BAND_CHUNKS * CHUNK
MAX_REL = 128
MEM_HEADS = 4
MEM_WIDTH = MEM_HEADS * HEAD_DIM
N_BRANCH = 3
D_FF = 2816
CONV_WIDTH = 3
RMS_EPS = 1e-6
NEG_INF = -1e30

O_POOL = 0
O_Q = O_POOL + POOL_WIDTH
O_K = O_Q + ATTN_WIDTH
O_V = O_K + ATTN_WIDTH
O_QM = O_V + ATTN_WIDTH
O_GATE = O_QM + MEM_WIDTH

LANES = 128
SUBLANES = 8
PAIR = 2 * HEAD_DIM
POOL_TAIL = 16
WIN_KEYS = BAND_KEYS + CHUNK
KV_CARRY = WIN_KEYS - CHUNK
UNION_KEYS = WIN_KEYS + 2 * CHUNK
KV_TAIL = CHUNK
SEQ_TILE = 512
MEM_ROWS = 128
FF_BLOCK = 256
CONV_PAD = SUBLANES
VMEM_LIMIT = 56 * 1024 * 1024


def _rms_norm(x, g):
    y = x * lax.rsqrt(jnp.mean(x * x, axis=-1, keepdims=True) + RMS_EPS)
    return y * g


def _dot(a, b):
    return jnp.dot(a, b, preferred_element_type=jnp.float32)


def _dot_nt(a, b):
    return lax.dot_general(a, b, (((1,), (1,)), ((), ())), preferred_element_type=jnp.float32)


def _softmax_pv(s, v):
    m = jnp.max(s, axis=-1, keepdims=True)
    e = jnp.exp(s - m)
    l = jnp.sum(e, axis=-1, keepdims=True)
    return _dot(e.astype(jnp.bfloat16), v) / l


def _mem_kv_kernel(mem_ref, g_ref, w_ref, k_ref, v_ref):
    mem_n = _rms_norm(mem_ref[...], g_ref[...]).astype(jnp.bfloat16)
    kv = _dot(mem_n, w_ref[...])
    k_ref[...] = kv[:, :MEM_WIDTH].astype(jnp.bfloat16)
    v_ref[...] = kv[:, MEM_WIDTH:].astype(jnp.bfloat16)


def _mixer_kernel(zero_ref, x_ref, g_ref, w_in_ref, b_gate_ref, w_pool_ref, pool_scale_ref, rel_vec_ref,
                  km_ref, vm_ref, w_up_pool_ref, w_up_attn_ref, w_up_mem_ref, w_out_ref,
                  o_ref,
                  h_buf, u_buf, q_buf, k_buf, v_buf, qm_buf, pool_buf, attn_buf, mem_buf, merged_buf,
                  part_buf, bias_buf, s_buf0, s_buf1, m_buf0, m_buf1, p_buf0, p_buf1):
    s_buf, m_buf, p_buf = (s_buf0, s_buf1), (m_buf0, m_buf1), (p_buf0, p_buf1)
    ts = x_ref.shape[0]
    tile = pl.program_id(0)

    @pl.when(tile == 0)
    def _init():
        u_buf[0:POOL_TAIL, :] = jnp.zeros((POOL_TAIL, POOL_WIDTH), jnp.float32)
        k_buf[...] = jnp.zeros(k_buf.shape, jnp.bfloat16)
        ext_lane = lax.broadcasted_iota(jnp.int32, v_buf.shape[1:], 1) % (2 * PAIR)
        for e in range(2):
            ones = jnp.logical_and(ext_lane >= PAIR + e * HEAD_DIM, ext_lane < PAIR + (e + 1) * HEAD_DIM)
            v_buf[e] = jnp.where(ones, 1.0, 0.0).astype(jnp.bfloat16)
        for buf in p_buf:
            buf[...] = jnp.zeros(buf.shape, jnp.bfloat16)
        win_lane = lax.broadcasted_iota(jnp.int32, (1, WIN_KEYS), 1)
        for h in range(ATTN_HEADS):
            vec = rel_vec_ref[h:h + 1, :]
            for qi in range(CHUNK):
                r = (h % 2) * CHUNK + qi
                even_row = pltpu.roll(vec, qi, axis=1) if qi else vec
                bias_buf[0, h // 2, r:r + 1, :] = jnp.where(win_lane < CHUNK, NEG_INF, even_row)
                odd_row = pltpu.roll(vec, qi + BAND_KEYS, axis=1)
                bias_buf[1, h // 2, r:r + 1, :] = jnp.where(win_lane >= BAND_KEYS, NEG_INF, odd_row)

    h_buf[...] = _rms_norm(x_ref[...], g_ref[...]).astype(jnp.bfloat16)

    u_buf[POOL_TAIL:POOL_TAIL + ts, :] = _dot(h_buf[...], w_in_ref[:, O_POOL:O_Q])
    def split_heads(q, even_ref, odd_ref):
        odd = (lax.broadcasted_iota(jnp.int32, q.shape, 1) // HEAD_DIM) % 2 == 1
        q = q * (HEAD_DIM ** -0.5)
        even_ref[...] = jnp.where(odd, 0.0, q).astype(jnp.bfloat16)
        odd_ref[...] = jnp.where(odd, q, 0.0).astype(jnp.bfloat16)

    split_heads(_dot(h_buf[...], w_in_ref[:, O_Q:O_K]), q_buf.at[0], q_buf.at[1])
    k_buf[KV_CARRY:KV_CARRY + ts, :] = _dot(h_buf[...], w_in_ref[:, O_K:O_V]).astype(jnp.bfloat16)
    v = _dot(h_buf[...], w_in_ref[:, O_V:O_QM])
    pair_lane_is_odd = lax.broadcasted_iota(jnp.int32, (ts, PAIR), 1) >= HEAD_DIM
    for p in range(ATTN_HEADS // 2):
        v_pair = v[:, p * PAIR:(p + 1) * PAIR]
        vcols = slice(p * 2 * PAIR, p * 2 * PAIR + PAIR)
        v_buf[0, KV_CARRY:KV_CARRY + ts, vcols] = jnp.where(pair_lane_is_odd, 0.0, v_pair).astype(jnp.bfloat16)
        v_buf[1, KV_CARRY:KV_CARRY + ts, vcols] = jnp.where(pair_lane_is_odd, v_pair, 0.0).astype(jnp.bfloat16)
    split_heads(_dot(h_buf[...], w_in_ref[:, O_QM:O_GATE]), qm_buf.at[0], qm_buf.at[1])

    u = u_buf[POOL_TAIL:POOL_TAIL + ts, :]
    lane_group = lax.broadcasted_iota(jnp.int32, (ts, POOL_WIDTH), 1) // POOL_GROUP_DIM
    run = u
    win_sum = jnp.zeros_like(u)
    for j in range(1, POOL_TAIL + 1):
        if j in POOL_WINDOWS:
            win_sum = jnp.where(lane_group == POOL_WINDOWS.index(j), run, win_sum)
        if j < POOL_TAIL:
            run = run + u_buf[POOL_TAIL - j:POOL_TAIL - j + ts, :]
    window = jnp.left_shift(2, lane_group)
    pos1 = tile * ts + lax.broadcasted_iota(jnp.int32, (ts, POOL_WIDTH), 0) + 1
    cnt = jnp.minimum(pos1, window).astype(jnp.float32)
    pooled = (win_sum / cnt - u).astype(jnp.bfloat16)
    pool_buf[...] = (_dot(pooled, w_pool_ref[...]) * pool_scale_ref[...]).astype(jnp.bfloat16)
    u_buf[0:POOL_TAIL, :] = u_buf[ts:ts + POOL_TAIL, :]

    def pair_cols(h):
        return slice((h // 2) * PAIR, (h // 2 + 1) * PAIR)

    mem_lane_is_odd = lax.broadcasted_iota(jnp.int32, (MEM_ROWS, PAIR), 1) >= HEAD_DIM

    def mem_body(r, carry):
        r0 = pl.multiple_of(r * MEM_ROWS, MEM_ROWS)
        scores = [_dot_nt(qm_buf[h % 2, pl.ds(r0, MEM_ROWS), pair_cols(h)], km_ref[:, pair_cols(h)])
                  for h in range(MEM_HEADS)]
        outs = [_softmax_pv(scores[h], vm_ref[:, pair_cols(h)]) for h in range(MEM_HEADS)]
        for h in range(0, MEM_HEADS, 2):
            mem_buf[pl.ds(r0, MEM_ROWS), pair_cols(h)] = jnp.where(mem_lane_is_odd, outs[h + 1], outs[h]).astype(jnp.bfloat16)
        return carry

    lax.fori_loop(0, ts // MEM_ROWS, mem_body, 0)

    merge_blk = 2 * LANES
    n_merge = D_MODEL // merge_blk

    def merge_cols(nb):
        return slice(nb * merge_blk, (nb + 1) * merge_blk)

    def gated(b, branch_buf, w_up_ref, nb):
        lo = b * D_MODEL + nb * merge_blk
        logits = _dot(h_buf[...], w_in_ref[:, O_GATE + lo:O_GATE + lo + merge_blk]) + b_gate_ref[:, lo:lo + merge_blk]
        return jax.nn.sigmoid(logits) * _dot(branch_buf[...], w_up_ref[:, merge_cols(nb)])

    win_col = lax.broadcasted_iota(jnp.int32, (1, WIN_KEYS), 1)
    pad_rows = jnp.where(tile == 0, KV_CARRY, 0)

    def ext_cols(p):
        return slice(p * 2 * PAIR, (p + 1) * 2 * PAIR)

    n_k = WIN_KEYS // LANES
    n_cp = ts // (2 * CHUNK)
    staged = zero_ref[0]

    def score_phase(cp):
        base = cp * 2 * CHUNK
        slot = cp % 2
        for j in range(2):
            q0 = base + j * CHUNK
            w0 = base + j * 2 * CHUNK
            pen = jnp.where(win_col + w0 < pad_rows, NEG_INF, 0.0)
            for p in range(ATTN_HEADS // 2):
                q_pair = jnp.concatenate([q_buf[0, q0:q0 + CHUNK, pair_cols(2 * p)],
                                          q_buf[1, q0:q0 + CHUNK, pair_cols(2 * p)]], axis=0)
                s = _dot_nt(q_pair, k_buf[w0:w0 + WIN_KEYS, pair_cols(2 * p)]) + bias_buf[j, p] + pen
                s_buf[slot][j, p] = s
                part = s[:, 0:LANES]
                for k in range(1, n_k):
                    part = jnp.maximum(part, s[:, k * LANES:(k + 1) * LANES])
                m_buf[slot][j, p] = jnp.broadcast_to(jnp.max(part, axis=-1, keepdims=True), (2 * CHUNK, LANES))

    def softmax_phase(cp):
        slot = cp % 2
        for j in range(2):
            rows = slice(j * CHUNK, (j + 1) * CHUNK)
            for p in range(ATTN_HEADS // 2):
                m = m_buf[slot][staged + j, p]
                for k in range(n_k):
                    e = jnp.exp(s_buf[slot][staged + j, p, :, k * LANES:(k + 1) * LANES] - m).astype(jnp.bfloat16)
                    lo = j * 2 * CHUNK + k * LANES
                    p_buf[slot][p, rows, lo:lo + LANES] = e[:CHUNK]
                    p_buf[slot][p, rows, UNION_KEYS + lo:UNION_KEYS + lo + LANES] = e[CHUNK:]

    def value_phase(cp):
        base = cp * 2 * CHUNK
        slot = cp % 2
        for p in range(ATTN_HEADS // 2):
            v_pair = jnp.concatenate([v_buf[0, base:base + UNION_KEYS, ext_cols(p)],
                                      v_buf[1, base:base + UNION_KEYS, ext_cols(p)]], axis=0)
            acc = _dot(p_buf[slot][staged + p], v_pair)
            attn_buf[base:base + 2 * CHUNK, pair_cols(2 * p)] = (acc[:, :PAIR] / acc[:, PAIR:]).astype(jnp.bfloat16)

    assert n_merge <= n_cp + 2
    for step in range(n_cp + 2):
        if step < n_cp:
            score_phase(step)
        if step < n_merge:
            part_buf[:, merge_cols(step)] = (gated(0, pool_buf, w_up_pool_ref, step)
                                             + gated(2, mem_buf, w_up_mem_ref, step))
        if 0 <= step - 1 < n_cp:
            softmax_phase(step - 1)
        if 0 <= step - 2 < n_cp:
            value_phase(step - 2)

    k_buf[CHUNK:KV_CARRY, :] = k_buf[ts + CHUNK:ts + KV_CARRY, :]
    for p in range(ATTN_HEADS // 2):
        vcols = slice(p * 2 * PAIR, p * 2 * PAIR + PAIR)
        v_buf[:, CHUNK:KV_CARRY, vcols] = v_buf[:, ts + CHUNK:ts + KV_CARRY, vcols]

    for nb in range(n_merge):
        merged = part_buf[:, merge_cols(nb)] + gated(1, attn_buf, w_up_attn_ref, nb)
        merged_buf[:, merge_cols(nb)] = merged.astype(jnp.bfloat16)
    o_ref[...] = x_ref[...] + _dot(merged_buf[...], w_out_ref[...])


def _ffn_kernel(x_ref, g_ref, w_up_ref, conv_w_ref, conv_b_ref, w_down_ref, g_final_ref, o_ref,
                h_buf, a_buf, tail_buf, hid_buf):
    ts = x_ref.shape[0]

    @pl.when(pl.program_id(0) == 0)
    def _init():
        tail_buf[...] = jnp.zeros_like(tail_buf)

    h_buf[...] = _rms_norm(x_ref[...], g_ref[...]).astype(jnp.bfloat16)

    def conv_up(cols):
        a = _dot(h_buf[...], w_up_ref[:, cols])
        a_buf[0:CONV_PAD, :] = tail_buf[:, cols]
        a_buf[CONV_PAD:CONV_PAD + ts, :] = a
        tail_buf[:, cols] = a_buf[ts:ts + CONV_PAD, :]
        out = conv_b_ref[:, cols] + a * conv_w_ref[CONV_WIDTH - 1:CONV_WIDTH, cols]
        for t in range(CONV_WIDTH - 1):
            back = CONV_WIDTH - 1 - t
            out = out + a_buf[CONV_PAD - back:CONV_PAD - back + ts, :] * conv_w_ref[t:t + 1, cols]
        return out

    for jb in range(D_FF // FF_BLOCK):
        gate = conv_up(slice(jb * FF_BLOCK, (jb + 1) * FF_BLOCK))
        val = conv_up(slice(D_FF + jb * FF_BLOCK, D_FF + (jb + 1) * FF_BLOCK))
        act = 0.5 * gate * (1.0 + lax.erf(gate * (2.0 ** -0.5)))
        hid_buf[:, jb * FF_BLOCK:(jb + 1) * FF_BLOCK] = (act * val).astype(jnp.bfloat16)

    y = x_ref[...] + _dot(hid_buf[...], w_down_ref[...])
    o_ref[...] = _rms_norm(y, g_final_ref[...])


def _resident(shape):
    zeros = (0,) * len(shape)
    return pl.BlockSpec(shape, lambda i: zeros, pipeline_mode=pl.Buffered(1))


def _rel_vector(rel_bias):
    assert CHUNK - 1 <= MAX_REL <= WIN_KEYS - CHUNK - 1
    n_clipped = WIN_KEYS - CHUNK - MAX_REL
    far = jnp.broadcast_to(rel_bias[:, 2 * MAX_REL:], (ATTN_HEADS, n_clipped))
    near = rel_bias[:, MAX_REL - (CHUNK - 1):2 * MAX_REL][:, ::-1]
    unused = jnp.zeros((ATTN_HEADS, 1), rel_bias.dtype)
    return jnp.concatenate([unused, far, near], axis=1).astype(jnp.float32)


def _block_diag(w):
    g, c, _ = w.shape
    eye = jnp.eye(g, dtype=w.dtype)
    return (eye[:, None, :, None] * w[:, :, None, :]).reshape(g * c, g * c)


def kernel(x, mem, norm_mix_g, norm_mem_g, w_in, b_gate, w_pool, pool_scale, rel_bias, w_mem_kv,
           w_up_pool, w_up_attn, w_up_mem, w_out, norm_ffn_g, w_ffn_up, conv_w, conv_b, w_ffn_down,
           norm_final_g):
    batch, seq, d = x.shape
    assert batch == 1 and d == D_MODEL and seq % SEQ_TILE == 0
    assert norm_mix_g.shape[0] == 1, "single layer"
    bf16 = jnp.bfloat16
    f32 = jnp.float32
    ts = SEQ_TILE
    n_tiles = seq // ts
    mem_len = mem.shape[1]
    row = lambda a: a.reshape(1, -1).astype(f32)
    params = pltpu.CompilerParams(dimension_semantics=("arbitrary",), vmem_limit_bytes=VMEM_LIMIT)

    km, vm = pl.pallas_call(
        _mem_kv_kernel,
        out_shape=(jax.ShapeDtypeStruct((mem_len, MEM_WIDTH), bf16),) * 2,
        name="mem_kv",
    )(mem[0], row(norm_mem_g[0]), w_mem_kv[0].astype(bf16))

    tile_spec = pl.BlockSpec((ts, D_MODEL), lambda i: (i, 0))
    mixer_inputs = (
        row(norm_mix_g[0]), w_in[0].astype(bf16), row(b_gate[0]), _block_diag(w_pool[0]).astype(bf16),
        row(pool_scale[0]), _rel_vector(rel_bias[0]), km, vm,
        w_up_pool[0].astype(bf16), w_up_attn[0].astype(bf16), w_up_mem[0].astype(bf16), w_out[0].astype(bf16),
    )
    x1 = pl.pallas_call(
        _mixer_kernel,
        grid=(n_tiles,),
        in_specs=[pl.BlockSpec(memory_space=pltpu.SMEM), tile_spec] + [_resident(a.shape) for a in mixer_inputs],
        out_specs=tile_spec,
        out_shape=jax.ShapeDtypeStruct((seq, D_MODEL), f32),
        scratch_shapes=[
            pltpu.VMEM((ts, D_MODEL), bf16),
            pltpu.VMEM((POOL_TAIL + ts, POOL_WIDTH), f32),
            pltpu.VMEM((2, ts, ATTN_WIDTH), bf16),
            pltpu.VMEM((KV_CARRY + ts + KV_TAIL, ATTN_WIDTH), bf16),
            pltpu.VMEM((2, KV_CARRY + ts + KV_TAIL, 2 * ATTN_WIDTH), bf16),
            pltpu.VMEM((2, ts, MEM_WIDTH), bf16),
            pltpu.VMEM((ts, POOL_WIDTH), bf16),
            pltpu.VMEM((ts, ATTN_WIDTH), bf16),
            pltpu.VMEM((ts, MEM_WIDTH), bf16),
            pltpu.VMEM((ts, D_MODEL), bf16),
            pltpu.VMEM((ts, D_MODEL), f32),
            pltpu.VMEM((2, ATTN_HEADS // 2, 2 * CHUNK, WIN_KEYS), f32),
            pltpu.VMEM((2, ATTN_HEADS // 2, 2 * CHUNK, WIN_KEYS), f32),
            pltpu.VMEM((2, ATTN_HEADS // 2, 2 * CHUNK, WIN_KEYS), f32),
            pltpu.VMEM((2, ATTN_HEADS // 2, 2 * CHUNK, LANES), f32),
            pltpu.VMEM((2, ATTN_HEADS // 2, 2 * CHUNK, LANES), f32),
            pltpu.VMEM((ATTN_HEADS // 2, 2 * CHUNK, 2 * UNION_KEYS), bf16),
            pltpu.VMEM((ATTN_HEADS // 2, 2 * CHUNK, 2 * UNION_KEYS), bf16),
        ],
        compiler_params=params,
        name="mixer",
    )(jnp.zeros((1,), jnp.int32), x[0], *mixer_inputs)

    ffn_inputs = (
        row(norm_ffn_g[0]), w_ffn_up[0].astype(bf16), conv_w[0].astype(f32), row(conv_b[0]),
        w_ffn_down[0].astype(bf16), row(norm_final_g),
    )
    out = pl.pallas_call(
        _ffn_kernel,
        grid=(n_tiles,),
        in_specs=[tile_spec] + [_resident(a.shape) for a in ffn_inputs],
        out_specs=tile_spec,
        out_shape=jax.ShapeDtypeStruct((seq, D_MODEL), x.dtype),
        scratch_shapes=[
            pltpu.VMEM((ts, D_MODEL), bf16),
            pltpu.VMEM((CONV_PAD + ts, FF_BLOCK), f32),
            pltpu.VMEM((CONV_PAD, 2 * D_FF), f32),
            pltpu.VMEM((ts, D_FF), bf16),
        ],
        compiler_params=params,
        name="ffn",
    )(x1, *ffn_inputs)
    return out[None]
```

```python
import functools

import jax
import jax.numpy as jnp
from jax import lax
from jax.experimental import pallas as pl
from jax.experimental.pallas import tpu as pltpu

D_MODEL = 1024
CHUNK = 64
HEAD_DIM = 64
POOL_WINDOWS = (2, 4, 8, 16)
POOL_GROUP_DIM = 64
POOL_WIDTH = len(POOL_WINDOWS) * POOL_GROUP_DIM
ATTN_HEADS = 8
ATTN_WIDTH = ATTN_HEADS * HEAD_DIM
BAND_CHUNKS = 9
BAND_KEYS = BAND_CHUNKS * CHUNK
MAX_REL = 128
MEM_HEADS = 4
MEM_WIDTH = MEM_HEADS * HEAD_DIM
N_BRANCH = 3
D_FF = 2816
CONV_WIDTH = 3
RMS_EPS = 1e-6
NEG_INF = -1e30

O_POOL = 0
O_Q = O_POOL + POOL_WIDTH
O_K = O_Q + ATTN_WIDTH
O_V = O_K + ATTN_WIDTH
O_QM = O_V + ATTN_WIDTH
O_GATE = O_QM + MEM_WIDTH

LANES = 128
SUBLANES = 8
PAIR = 2 * HEAD_DIM
POOL_TAIL = 16
WIN_KEYS = BAND_KEYS + CHUNK
KV_CARRY = WIN_KEYS - CHUNK
UNION_KEYS = WIN_KEYS + 2 * CHUNK
KV_TAIL = CHUNK
SEQ_TILE = 512
FFN_TILE = 1024
FFN_SUB = 1024
MEM_ROWS = 128
FF_BLOCK = 256
CONV_PAD = SUBLANES
VMEM_LIMIT = 56 * 1024 * 1024


def _rms_norm(x, g):
    y = x * lax.rsqrt(jnp.mean(x * x, axis=-1, keepdims=True) + RMS_EPS)
    return y * g


def _dot(a, b):
    return jnp.dot(a, b, preferred_element_type=jnp.float32)


def _dot_nt(a, b):
    return lax.dot_general(a, b, (((1,), (1,)), ((), ())), preferred_element_type=jnp.float32)


def _mem_kv_kernel(mem_ref, g_ref, w_ref, k_ref, v_ref):
    mem_len = mem_ref.shape[0]
    mem_n = _rms_norm(mem_ref[...], g_ref[...]).astype(jnp.bfloat16)
    kv = _dot(mem_n, w_ref[...])
    k_ref[...] = kv[:, :MEM_WIDTH].astype(jnp.bfloat16)
    lane_is_odd = lax.broadcasted_iota(jnp.int32, (mem_len, PAIR), 1) >= HEAD_DIM
    for p in range(MEM_HEADS // 2):
        v_pair = kv[:, MEM_WIDTH + p * PAIR:MEM_WIDTH + (p + 1) * PAIR]
        for e in range(2):
            keep = lane_is_odd if e else jnp.logical_not(lane_is_odd)
            rows = slice(e * mem_len, (e + 1) * mem_len)
            v_ref[p, rows, 0:PAIR] = jnp.where(keep, v_pair, 0.0).astype(jnp.bfloat16)
            v_ref[p, rows, PAIR:2 * PAIR] = jnp.where(keep, 1.0, 0.0).astype(jnp.bfloat16)


def _mixer_kernel(zero_ref, x_ref, g_ref, w_in_ref, b_gate_ref, w_pool_ref, pool_scale_ref, rel_vec_ref,
                  km_ref, vm_ref, w_up_pool_ref, w_up_attn_ref, w_up_mem_ref, w_out_ref,
                  o_ref,
                  h_buf, u_buf, q_buf, k_buf, v_buf, qm_buf, pool_buf, attn_buf, mem_buf, merged_buf,
                  part_buf, pm_buf, bias_buf, s_buf0, s_buf1, m_buf0, m_buf1, p_buf0, p_buf1):
    s_buf, m_buf, p_buf = (s_buf0, s_buf1), (m_buf0, m_buf1), (p_buf0, p_buf1)
    ts = x_ref.shape[0]
    tile = pl.program_id(0)

    @pl.when(tile == 0)
    def _init():
        u_buf[0:POOL_TAIL, :] = jnp.zeros((POOL_TAIL, POOL_WIDTH), jnp.float32)
        k_buf[...] = jnp.zeros(k_buf.shape, jnp.bfloat16)
        ext_lane = lax.broadcasted_iota(jnp.int32, v_buf.shape[1:], 1) % (2 * PAIR)
        for e in range(2):
            ones = jnp.logical_and(ext_lane >= PAIR + e * HEAD_DIM, ext_lane < PAIR + (e + 1) * HEAD_DIM)
            v_buf[e] = jnp.where(ones, 1.0, 0.0).astype(jnp.bfloat16)
        for buf in p_buf:
            buf[...] = jnp.zeros(buf.shape, jnp.bfloat16)
        win_lane = lax.broadcasted_iota(jnp.int32, (1, WIN_KEYS), 1)
        for h in range(ATTN_HEADS):
            vec = rel_vec_ref[h:h + 1, :]
            for qi in range(CHUNK):
                r = (h % 2) * CHUNK + qi
                even_row = pltpu.roll(vec, qi, axis=1) if qi else vec
                bias_buf[0, h // 2, r:r + 1, :] = jnp.where(win_lane < CHUNK, NEG_INF, even_row)
                odd_row = pltpu.roll(vec, qi + BAND_KEYS, axis=1)
                bias_buf[1, h // 2, r:r + 1, :] = jnp.where(win_lane >= BAND_KEYS, NEG_INF, odd_row)

    h_buf[...] = _rms_norm(x_ref[...], g_ref[...]).astype(jnp.bfloat16)

    u_buf[POOL_TAIL:POOL_TAIL + ts, :] = _dot(h_buf[...], w_in_ref[:, O_POOL:O_Q])
    def split_heads(q, even_ref, odd_ref):
        odd = (lax.broadcasted_iota(jnp.int32, q.shape, 1) // HEAD_DIM) % 2 == 1
        q = q * (HEAD_DIM ** -0.5)
        even_ref[...] = jnp.where(odd, 0.0, q).astype(jnp.bfloat16)
        odd_ref[...] = jnp.where(odd, q, 0.0).astype(jnp.bfloat16)

    split_heads(_dot(h_buf[...], w_in_ref[:, O_Q:O_K]), q_buf.at[0], q_buf.at[1])
    k_buf[KV_CARRY:KV_CARRY + ts, :] = _dot(h_buf[...], w_in_ref[:, O_K:O_V]).astype(jnp.bfloat16)
    v = _dot(h_buf[...], w_in_ref[:, O_V:O_QM])
    pair_lane_is_odd = lax.broadcasted_iota(jnp.int32, (ts, PAIR), 1) >= HEAD_DIM
    for p in range(ATTN_HEADS // 2):
        v_pair = v[:, p * PAIR:(p + 1) * PAIR]
        vcols = slice(p * 2 * PAIR, p * 2 * PAIR + PAIR)
        v_buf[0, KV_CARRY:KV_CARRY + ts, vcols] = jnp.where(pair_lane_is_odd, 0.0, v_pair).astype(jnp.bfloat16)
        v_buf[1, KV_CARRY:KV_CARRY + ts, vcols] = jnp.where(pair_lane_is_odd, v_pair, 0.0).astype(jnp.bfloat16)
    split_heads(_dot(h_buf[...], w_in_ref[:, O_QM:O_GATE]), qm_buf.at[0], qm_buf.at[1])

    u = u_buf[POOL_TAIL:POOL_TAIL + ts, :]
    lane_group = lax.broadcasted_iota(jnp.int32, (ts, POOL_WIDTH), 1) // POOL_GROUP_DIM
    run = u
    win_sum = jnp.zeros_like(u)
    for j in range(1, POOL_TAIL + 1):
        if j in POOL_WINDOWS:
            win_sum = jnp.where(lane_group == POOL_WINDOWS.index(j), run, win_sum)
        if j < POOL_TAIL:
            run = run + u_buf[POOL_TAIL - j:POOL_TAIL - j + ts, :]
    window = jnp.left_shift(2, lane_group)
    pos1 = tile * ts + lax.broadcasted_iota(jnp.int32, (ts, POOL_WIDTH), 0) + 1
    cnt = jnp.minimum(pos1, window).astype(jnp.float32)
    pooled = (win_sum / cnt - u).astype(jnp.bfloat16)
    pool_buf[...] = (_dot(pooled, w_pool_ref[...]) * pool_scale_ref[...]).astype(jnp.bfloat16)
    u_buf[0:POOL_TAIL, :] = u_buf[ts:ts + POOL_TAIL, :]

    def pair_cols(h):
        return slice((h // 2) * PAIR, (h // 2 + 1) * PAIR)

    staged = zero_ref[0]

    mem_len = km_ref.shape[0]
    n_mem = ts // MEM_ROWS

    def mem_score(r):
        r0 = r * MEM_ROWS
        for h in range(MEM_HEADS):
            s = _dot_nt(qm_buf[h % 2, r0:r0 + MEM_ROWS, pair_cols(h)], km_ref[:, pair_cols(h)])
            e = jnp.exp(s - jnp.max(s, axis=-1, keepdims=True))
            pm_buf[r, h // 2, :, (h % 2) * mem_len:(h % 2 + 1) * mem_len] = e.astype(jnp.bfloat16)

    def mem_value(r):
        r0 = r * MEM_ROWS
        for p in range(MEM_HEADS // 2):
            acc = _dot(pm_buf[staged + r, p], vm_ref[p])
            mem_buf[r0:r0 + MEM_ROWS, pair_cols(2 * p)] = (acc[:, :PAIR] / acc[:, PAIR:]).astype(jnp.bfloat16)

    merge_blk = 2 * LANES
    n_merge = D_MODEL // merge_blk

    def merge_cols(nb):
        return slice(nb * merge_blk, (nb + 1) * merge_blk)

    def gated(b, branch_buf, w_up_ref, nb):
        lo = b * D_MODEL + nb * merge_blk
        logits = _dot(h_buf[...], w_in_ref[:, O_GATE + lo:O_GATE + lo + merge_blk]) + b_gate_ref[:, lo:lo + merge_blk]
        return jax.nn.sigmoid(logits) * _dot(branch_buf[...], w_up_ref[:, merge_cols(nb)])

    win_col = lax.broadcasted_iota(jnp.int32, (1, WIN_KEYS), 1)
    pad_rows = jnp.where(tile == 0, KV_CARRY, 0)

    def ext_cols(p):
        return slice(p * 2 * PAIR, (p + 1) * 2 * PAIR)

    n_k = WIN_KEYS // LANES
    n_cp = ts // (2 * CHUNK)

    def score_phase(cp):
        base = cp * 2 * CHUNK
        slot = cp % 2
        for j in range(2):
            q0 = base + j * CHUNK
            w0 = base + j * 2 * CHUNK
            pen = jnp.where(win_col + w0 < pad_rows, NEG_INF, 0.0)
            for p in range(ATTN_HEADS // 2):
                q_pair = jnp.concatenate([q_buf[0, q0:q0 + CHUNK, pair_cols(2 * p)],
                                          q_buf[1, q0:q0 + CHUNK, pair_cols(2 * p)]], axis=0)
                s = _dot_nt(q_pair, k_buf[w0:w0 + WIN_KEYS, pair_cols(2 * p)]) + bias_buf[j, p] + pen
                s_buf[slot][j, p] = s
                part = s[:, 0:LANES]
                for k in range(1, n_k):
                    part = jnp.maximum(part, s[:, k * LANES:(k + 1) * LANES])
                m_buf[slot][j, p] = jnp.broadcast_to(jnp.max(part, axis=-1, keepdims=True), (2 * CHUNK, LANES))

    def softmax_phase(cp):
        slot = cp % 2
        for j in range(2):
            rows = slice(j * CHUNK, (j + 1) * CHUNK)
            for p in range(ATTN_HEADS // 2):
                m = m_buf[slot][staged + j, p]
                for k in range(n_k):
                    e = jnp.exp(s_buf[slot][staged + j, p, :, k * LANES:(k + 1) * LANES] - m).astype(jnp.bfloat16)
                    lo = j * 2 * CHUNK + k * LANES
                    p_buf[slot][p, rows, lo:lo + LANES] = e[:CHUNK]
                    p_buf[slot][p, rows, UNION_KEYS + lo:UNION_KEYS + lo + LANES] = e[CHUNK:]

    def value_phase(cp):
        base = cp * 2 * CHUNK
        slot = cp % 2
        for p in range(ATTN_HEADS // 2):
            v_pair = jnp.concatenate([v_buf[0, base:base + UNION_KEYS, ext_cols(p)],
                                      v_buf[1, base:base + UNION_KEYS, ext_cols(p)]], axis=0)
            acc = _dot(p_buf[slot][staged + p], v_pair)
            attn_buf[base:base + 2 * CHUNK, pair_cols(2 * p)] = (acc[:, :PAIR] / acc[:, PAIR:]).astype(jnp.bfloat16)

    mem_steps = 2
    mem_per_step = n_mem // mem_steps
    assert mem_steps + n_merge <= n_cp + 2
    for step in range(n_cp + 2):
        mem_blocks = range(step * mem_per_step, (step + 1) * mem_per_step) if step < mem_steps else ()
        for r in mem_blocks:
            mem_score(r)
        if step < n_cp:
            score_phase(step)
        if 0 <= step - mem_steps < n_merge:
            nb = step - mem_steps
            part_buf[:, merge_cols(nb)] = (gated(0, pool_buf, w_up_pool_ref, nb)
                                           + gated(2, mem_buf, w_up_mem_ref, nb))
        if 0 <= step - 1 < n_cp:
            softmax_phase(step - 1)
        if 0 <= step - 2 < n_cp:
            value_phase(step - 2)
        for r in mem_blocks:
            mem_value(r)

    k_buf[CHUNK:KV_CARRY, :] = k_buf[ts + CHUNK:ts + KV_CARRY, :]
    for p in range(ATTN_HEADS // 2):
        vcols = slice(p * 2 * PAIR, p * 2 * PAIR + PAIR)
        v_buf[:, CHUNK:KV_CARRY, vcols] = v_buf[:, ts + CHUNK:ts + KV_CARRY, vcols]

    for nb in range(n_merge):
        merged = part_buf[:, merge_cols(nb)] + gated(1, attn_buf, w_up_attn_ref, nb)
        merged_buf[:, merge_cols(nb)] = merged.astype(jnp.bfloat16)
    o_ref[...] = x_ref[...] + _dot(merged_buf[...], w_out_ref[...])


def _ffn_kernel(x_ref, g_ref, w_up_ref, conv_w_ref, conv_b_ref, w_down_ref, g_final_ref, o_ref,
                h_buf, a_buf, tail_buf, hid_buf):
    sub = a_buf.shape[0] - CONV_PAD

    @pl.when(pl.program_id(0) == 0)
    def _init():
        tail_buf[...] = jnp.zeros_like(tail_buf)

    def conv_up(rows, cols):
        a = _dot(h_buf[rows, :], w_up_ref[:, cols])
        a_buf[0:CONV_PAD, :] = tail_buf[:, cols]
        a_buf[CONV_PAD:CONV_PAD + sub, :] = a
        tail_buf[:, cols] = a_buf[sub:sub + CONV_PAD, :]
        out = conv_b_ref[:, cols] + a * conv_w_ref[CONV_WIDTH - 1:CONV_WIDTH, cols]
        for t in range(CONV_WIDTH - 1):
            back = CONV_WIDTH - 1 - t
            out = out + a_buf[CONV_PAD - back:CONV_PAD - back + sub, :] * conv_w_ref[t:t + 1, cols]
        return out

    for r0 in range(0, x_ref.shape[0], sub):
        rows = slice(r0, r0 + sub)
        h_buf[rows, :] = _rms_norm(x_ref[rows, :], g_ref[...]).astype(jnp.bfloat16)
        for jb in range(D_FF // FF_BLOCK):
            gate = conv_up(rows, slice(jb * FF_BLOCK, (jb + 1) * FF_BLOCK))
            val = conv_up(rows, slice(D_FF + jb * FF_BLOCK, D_FF + (jb + 1) * FF_BLOCK))
            act = 0.5 * gate * (1.0 + lax.erf(gate * (2.0 ** -0.5)))
            hid_buf[rows, jb * FF_BLOCK:(jb + 1) * FF_BLOCK] = (act * val).astype(jnp.bfloat16)
        y = x_ref[rows, :] + _dot(hid_buf[rows, :], w_down_ref[...])
        o_ref[rows, :] = _rms_norm(y, g_final_ref[...])


def _resident(shape):
    zeros = (0,) * len(shape)
    return pl.BlockSpec(shape, lambda i: zeros, pipeline_mode=pl.Buffered(1))


def _rel_vector(rel_bias):
    assert CHUNK - 1 <= MAX_REL <= WIN_KEYS - CHUNK - 1
    n_clipped = WIN_KEYS - CHUNK - MAX_REL
    far = jnp.broadcast_to(rel_bias[:, 2 * MAX_REL:], (ATTN_HEADS, n_clipped))
    near = rel_bias[:, MAX_REL - (CHUNK - 1):2 * MAX_REL][:, ::-1]
    unused = jnp.zeros((ATTN_HEADS, 1), rel_bias.dtype)
    return jnp.concatenate([unused, far, near], axis=1).astype(jnp.float32)


def _block_diag(w):
    g, c, _ = w.shape
    eye = jnp.eye(g, dtype=w.dtype)
    return (eye[:, None, :, None] * w[:, :, None, :]).reshape(g * c, g * c)


def kernel(x, mem, norm_mix_g, norm_mem_g, w_in, b_gate, w_pool, pool_scale, rel_bias, w_mem_kv,
           w_up_pool, w_up_attn, w_up_mem, w_out, norm_ffn_g, w_ffn_up, conv_w, conv_b, w_ffn_down,
           norm_final_g):
    batch, seq, d = x.shape
    assert batch == 1 and d == D_MODEL and seq % SEQ_TILE == 0
    assert norm_mix_g.shape[0] == 1, "single layer"
    bf16 = jnp.bfloat16
    f32 = jnp.float32
    ts = SEQ_TILE
    n_tiles = seq // ts
    mem_len = mem.shape[1]
    row = lambda a: a.reshape(1, -1).astype(f32)
    params = pltpu.CompilerParams(dimension_semantics=("arbitrary",), vmem_limit_bytes=VMEM_LIMIT)

    km, vm = pl.pallas_call(
        _mem_kv_kernel,
        out_shape=(jax.ShapeDtypeStruct((mem_len, MEM_WIDTH), bf16),
                   jax.ShapeDtypeStruct((MEM_HEADS // 2, 2 * mem_len, 2 * PAIR), bf16)),
        name="mem_kv",
    )(mem[0], row(norm_mem_g[0]), w_mem_kv[0].astype(bf16))

    tile_spec = pl.BlockSpec((ts, D_MODEL), lambda i: (i, 0))
    mixer_inputs = (
        row(norm_mix_g[0]), w_in[0].astype(bf16), row(b_gate[0]), _block_diag(w_pool[0]).astype(bf16),
        row(pool_scale[0]), _rel_vector(rel_bias[0]), km, vm,
        w_up_pool[0].astype(bf16), w_up_attn[0].astype(bf16), w_up_mem[0].astype(bf16), w_out[0].astype(bf16),
    )
    x1 = pl.pallas_call(
        _mixer_kernel,
        grid=(n_tiles,),
        in_specs=[pl.BlockSpec(memory_space=pltpu.SMEM), tile_spec] + [_resident(a.shape) for a in mixer_inputs],
        out_specs=tile_spec,
        out_shape=jax.ShapeDtypeStruct((seq, D_MODEL), f32),
        scratch_shapes=[
            pltpu.VMEM((ts, D_MODEL), bf16),
            pltpu.VMEM((POOL_TAIL + ts, POOL_WIDTH), f32),
            pltpu.VMEM((2, ts, ATTN_WIDTH), bf16),
            pltpu.VMEM((KV_CARRY + ts + KV_TAIL, ATTN_WIDTH), bf16),
            pltpu.VMEM((2, KV_CARRY + ts + KV_TAIL, 2 * ATTN_WIDTH), bf16),
            pltpu.VMEM((2, ts, MEM_WIDTH), bf16),
            pltpu.VMEM((ts, POOL_WIDTH), bf16),
            pltpu.VMEM((ts, ATTN_WIDTH), bf16),
            pltpu.VMEM((ts, MEM_WIDTH), bf16),
            pltpu.VMEM((ts, D_MODEL), bf16),
            pltpu.VMEM((ts, D_MODEL), f32),
            pltpu.VMEM((ts // MEM_ROWS, MEM_HEADS // 2, MEM_ROWS, 2 * mem_len), bf16),
            pltpu.VMEM((2, ATTN_HEADS // 2, 2 * CHUNK, WIN_KEYS), f32),
            pltpu.VMEM((2, ATTN_HEADS // 2, 2 * CHUNK, WIN_KEYS), f32),
            pltpu.VMEM((2, ATTN_HEADS // 2, 2 * CHUNK, WIN_KEYS), f32),
            pltpu.VMEM((2, ATTN_HEADS // 2, 2 * CHUNK, LANES), f32),
            pltpu.VMEM((2, ATTN_HEADS // 2, 2 * CHUNK, LANES), f32),
            pltpu.VMEM((ATTN_HEADS // 2, 2 * CHUNK, 2 * UNION_KEYS), bf16),
            pltpu.VMEM((ATTN_HEADS // 2, 2 * CHUNK, 2 * UNION_KEYS), bf16),
        ],
        compiler_params=params,
        name="mixer",
    )(jnp.zeros((1,), jnp.int32), x[0], *mixer_inputs)

    ffn_inputs = (
        row(norm_ffn_g[0]), w_ffn_up[0].astype(bf16), conv_w[0].astype(f32), row(conv_b[0]),
        w_ffn_down[0].astype(bf16), row(norm_final_g),
    )
    tf = FFN_TILE
    ffn_tile_spec = pl.BlockSpec((tf, D_MODEL), lambda i: (i, 0))
    out = pl.pallas_call(
        _ffn_kernel,
        grid=(seq // tf,),
        in_specs=[ffn_tile_spec] + [_resident(a.shape) for a in ffn_inputs],
        out_specs=ffn_tile_spec,
        out_shape=jax.ShapeDtypeStruct((seq, D_MODEL), x.dtype),
        scratch_shapes=[
            pltpu.VMEM((tf, D_MODEL), bf16),
            pltpu.VMEM((CONV_PAD + FFN_SUB, FF_BLOCK), f32),
            pltpu.VMEM((CONV_PAD, 2 * D_FF), f32),
            pltpu.VMEM((tf, D_FF), bf16),
        ],
        compiler_params=params,
        name="ffn",
    )(x1, *ffn_inputs)
    return out[None]
```

```python
import functools

import jax
import jax.numpy as jnp
from jax import lax
from jax.experimental import pallas as pl
from jax.experimental.pallas import tpu as pltpu

D_MODEL = 1024
CHUNK = 64
HEAD_DIM = 64
POOL_WINDOWS = (2, 4, 8, 16)
POOL_GROUP_DIM = 64
POOL_WIDTH = len(POOL_WINDOWS) * POOL_GROUP_DIM
ATTN_HEADS = 8
ATTN_WIDTH = ATTN_HEADS * HEAD_DIM
BAND_CHUNKS = 9
BAND_KEYS = BAND_CHUNKS * CHUNK
MAX_REL = 128
MEM_HEADS = 4
MEM_WIDTH = MEM_HEADS * HEAD_DIM
N_BRANCH = 3
D_FF = 2816
CONV_WIDTH = 3
RMS_EPS = 1e-6
NEG_INF = -1e30

O_POOL = 0
O_Q = O_POOL + POOL_WIDTH
O_K = O_Q + ATTN_WIDTH
O_V = O_K + ATTN_WIDTH
O_QM = O_V + ATTN_WIDTH
O_GATE = O_QM + MEM_WIDTH

LANES = 128
SUBLANES = 8
PAIR = 2 * HEAD_DIM
POOL_TAIL = 16
WIN_KEYS = BAND_KEYS + CHUNK
KV_CARRY = WIN_KEYS - CHUNK
UNION_KEYS = WIN_KEYS + 2 * CHUNK
GROUP_KEYS = WIN_KEYS + 4 * CHUNK
KV_TAIL = CHUNK
SEQ_TILE = 512
FFN_TILE = 1024
FFN_SUB = 1024
MEM_ROWS = 128
FF_BLOCK = 256
CONV_PAD = SUBLANES
VMEM_LIMIT = 60 * 1024 * 1024


def _rms_norm(x, g):
    y = x * lax.rsqrt(jnp.mean(x * x, axis=-1, keepdims=True) + RMS_EPS)
    return y * g


def _dot(a, b):
    return jnp.dot(a, b, preferred_element_type=jnp.float32)


def _dot_nt(a, b):
    return lax.dot_general(a, b, (((1,), (1,)), ((), ())), preferred_element_type=jnp.float32)


def _mem_kv_kernel(mem_ref, g_ref, w_ref, k_ref, v_ref):
    mem_len = mem_ref.shape[0]
    mem_n = _rms_norm(mem_ref[...], g_ref[...]).astype(jnp.bfloat16)
    kv = _dot(mem_n, w_ref[...])
    k_ref[...] = kv[:, :MEM_WIDTH].astype(jnp.bfloat16)
    lane_is_odd = lax.broadcasted_iota(jnp.int32, (mem_len, PAIR), 1) >= HEAD_DIM
    for p in range(MEM_HEADS // 2):
        v_pair = kv[:, MEM_WIDTH + p * PAIR:MEM_WIDTH + (p + 1) * PAIR]
        for e in range(2):
            keep = lane_is_odd if e else jnp.logical_not(lane_is_odd)
            rows = slice(e * mem_len, (e + 1) * mem_len)
            v_ref[p, rows, 0:PAIR] = jnp.where(keep, v_pair, 0.0).astype(jnp.bfloat16)
            v_ref[p, rows, PAIR:2 * PAIR] = jnp.where(keep, 1.0, 0.0).astype(jnp.bfloat16)


def _mixer_kernel(zero_ref, x_ref, g_ref, w_in_ref, b_gate_ref, w_pool_ref, pool_scale_ref, rel_vec_ref,
                  km_ref, vm_ref, w_up_pool_ref, w_up_attn_ref, w_up_mem_ref, w_out_ref,
                  o_ref,
                  h_buf, u_buf, q_buf, k_buf, v_buf, qm_buf, pool_buf, attn_buf, mem_buf, merged_buf,
                  part_buf, pm_buf, bias_buf, s_buf0, s_buf1, m_buf0, m_buf1, p_buf0, p_buf1):
    s_buf, m_buf, p_buf = (s_buf0, s_buf1), (m_buf0, m_buf1), (p_buf0, p_buf1)
    ts = x_ref.shape[0]
    tile = pl.program_id(0)

    @pl.when(tile == 0)
    def _init():
        u_buf[0:POOL_TAIL, :] = jnp.zeros((POOL_TAIL, POOL_WIDTH), jnp.float32)
        k_buf[...] = jnp.zeros(k_buf.shape, jnp.bfloat16)
        ext_lane = lax.broadcasted_iota(jnp.int32, v_buf.shape[1:], 1) % (2 * PAIR)
        for e in range(2):
            ones = jnp.logical_and(ext_lane >= PAIR + e * HEAD_DIM, ext_lane < PAIR + (e + 1) * HEAD_DIM)
            v_buf[e] = jnp.where(ones, 1.0, 0.0).astype(jnp.bfloat16)
        for buf in p_buf:
            buf[...] = jnp.zeros(buf.shape, jnp.bfloat16)
        win_lane = lax.broadcasted_iota(jnp.int32, (1, WIN_KEYS), 1)
        for h in range(ATTN_HEADS):
            vec = rel_vec_ref[h:h + 1, :]
            for qi in range(CHUNK):
                r = (h % 2) * CHUNK + qi
                even_row = pltpu.roll(vec, qi, axis=1) if qi else vec
                bias_buf[0, h // 2, r:r + 1, :] = jnp.where(win_lane < CHUNK, NEG_INF, even_row)
                odd_row = pltpu.roll(vec, qi + BAND_KEYS, axis=1)
                bias_buf[1, h // 2, r:r + 1, :] = jnp.where(win_lane >= BAND_KEYS, NEG_INF, odd_row)

    h_buf[...] = _rms_norm(x_ref[...], g_ref[...]).astype(jnp.bfloat16)

    u_buf[POOL_TAIL:POOL_TAIL + ts, :] = _dot(h_buf[...], w_in_ref[:, O_POOL:O_Q])
    def split_heads(q, even_ref, odd_ref):
        odd = (lax.broadcasted_iota(jnp.int32, q.shape, 1) // HEAD_DIM) % 2 == 1
        q = q * (HEAD_DIM ** -0.5)
        even_ref[...] = jnp.where(odd, 0.0, q).astype(jnp.bfloat16)
        odd_ref[...] = jnp.where(odd, q, 0.0).astype(jnp.bfloat16)

    split_heads(_dot(h_buf[...], w_in_ref[:, O_Q:O_K]), q_buf.at[0], q_buf.at[1])
    k_buf[KV_CARRY:KV_CARRY + ts, :] = _dot(h_buf[...], w_in_ref[:, O_K:O_V]).astype(jnp.bfloat16)
    v = _dot(h_buf[...], w_in_ref[:, O_V:O_QM])
    pair_lane_is_odd = lax.broadcasted_iota(jnp.int32, (ts, PAIR), 1) >= HEAD_DIM
    for p in range(ATTN_HEADS // 2):
        v_pair = v[:, p * PAIR:(p + 1) * PAIR]
        vcols = slice(p * 2 * PAIR, p * 2 * PAIR + PAIR)
        v_buf[0, KV_CARRY:KV_CARRY + ts, vcols] = jnp.where(pair_lane_is_odd, 0.0, v_pair).astype(jnp.bfloat16)
        v_buf[1, KV_CARRY:KV_CARRY + ts, vcols] = jnp.where(pair_lane_is_odd, v_pair, 0.0).astype(jnp.bfloat16)
    split_heads(_dot(h_buf[...], w_in_ref[:, O_QM:O_GATE]), qm_buf.at[0], qm_buf.at[1])

    u = u_buf[POOL_TAIL:POOL_TAIL + ts, :]
    lane_group = lax.broadcasted_iota(jnp.int32, (ts, POOL_WIDTH), 1) // POOL_GROUP_DIM
    run = u
    win_sum = jnp.zeros_like(u)
    for j in range(1, POOL_TAIL + 1):
        if j in POOL_WINDOWS:
            win_sum = jnp.where(lane_group == POOL_WINDOWS.index(j), run, win_sum)
        if j < POOL_TAIL:
            run = run + u_buf[POOL_TAIL - j:POOL_TAIL - j + ts, :]
    window = jnp.left_shift(2, lane_group)
    pos1 = tile * ts + lax.broadcasted_iota(jnp.int32, (ts, POOL_WIDTH), 0) + 1
    cnt = jnp.minimum(pos1, window).astype(jnp.float32)
    pooled = (win_sum / cnt - u).astype(jnp.bfloat16)
    pool_buf[...] = (_dot(pooled, w_pool_ref[...]) * pool_scale_ref[...]).astype(jnp.bfloat16)
    u_buf[0:POOL_TAIL, :] = u_buf[ts:ts + POOL_TAIL, :]

    def pair_cols(h):
        return slice((h // 2) * PAIR, (h // 2 + 1) * PAIR)

    staged = zero_ref[0]

    mem_len = km_ref.shape[0]
    n_mem = ts // MEM_ROWS

    def mem_score(r):
        r0 = r * MEM_ROWS
        for h in range(MEM_HEADS):
            s = _dot_nt(qm_buf[h % 2, r0:r0 + MEM_ROWS, pair_cols(h)], km_ref[:, pair_cols(h)])
            e = jnp.exp(s - jnp.max(s, axis=-1, keepdims=True))
            pm_buf[r, h // 2, :, (h % 2) * mem_len:(h % 2 + 1) * mem_len] = e.astype(jnp.bfloat16)

    def mem_value(r):
        r0 = r * MEM_ROWS
        for p in range(MEM_HEADS // 2):
            acc = _dot(pm_buf[staged + r, p], vm_ref[p])
            mem_buf[r0:r0 + MEM_ROWS, pair_cols(2 * p)] = (acc[:, :PAIR] / acc[:, PAIR:]).astype(jnp.bfloat16)

    merge_blk = 2 * LANES
    n_merge = D_MODEL // merge_blk

    def merge_cols(nb):
        return slice(nb * merge_blk, (nb + 1) * merge_blk)

    def gated(b, branch_buf, w_up_ref, nb):
        lo = b * D_MODEL + nb * merge_blk
        logits = _dot(h_buf[...], w_in_ref[:, O_GATE + lo:O_GATE + lo + merge_blk]) + b_gate_ref[:, lo:lo + merge_blk]
        return jax.nn.sigmoid(logits) * _dot(branch_buf[...], w_up_ref[:, merge_cols(nb)])

    win_col = lax.broadcasted_iota(jnp.int32, (1, WIN_KEYS), 1)
    pad_rows = jnp.where(tile == 0, KV_CARRY, 0)

    def ext_cols(p):
        return slice(p * 2 * PAIR, (p + 1) * 2 * PAIR)

    n_k = WIN_KEYS // LANES
    n_cp = ts // (2 * CHUNK)
    n_grp = ts // (4 * CHUNK)

    def win_off(j):
        return ((j + 1) // 2) * 2 * CHUNK

    def score_phase(cp):
        base = cp * 2 * CHUNK
        slot = cp % 2
        for p in range(ATTN_HEADS // 2):
            q_rows = [q_buf[e, base + j * CHUNK:base + (j + 1) * CHUNK, pair_cols(2 * p)]
                      for j in range(2) for e in range(2)]
            s_all = _dot_nt(jnp.concatenate(q_rows, axis=0), k_buf[base:base + UNION_KEYS, pair_cols(2 * p)])
            for j in range(2):
                pen = jnp.where(win_col + (base + win_off(j)) < pad_rows, NEG_INF, 0.0)
                s = s_all[j * 2 * CHUNK:(j + 1) * 2 * CHUNK, win_off(j):win_off(j) + WIN_KEYS] + bias_buf[j, p] + pen
                s_buf[slot][j, p] = s
                part = s[:, 0:LANES]
                for k in range(1, n_k):
                    part = jnp.maximum(part, s[:, k * LANES:(k + 1) * LANES])
                m_buf[slot][j, p] = jnp.broadcast_to(jnp.max(part, axis=-1, keepdims=True), (2 * CHUNK, LANES))

    def softmax_phase(cp):
        slot = cp % 2
        grp_buf = p_buf[(cp // 2) % 2]
        for j in range(2):
            jj = 2 * (cp % 2) + j
            rows = slice(jj * CHUNK, (jj + 1) * CHUNK)
            for p in range(ATTN_HEADS // 2):
                m = m_buf[slot][staged + j, p]
                for k in range(n_k):
                    e = jnp.exp(s_buf[slot][staged + j, p, :, k * LANES:(k + 1) * LANES] - m).astype(jnp.bfloat16)
                    lo = win_off(jj) + k * LANES
                    grp_buf[p, rows, lo:lo + LANES] = e[:CHUNK]
                    grp_buf[p, rows, GROUP_KEYS + lo:GROUP_KEYS + lo + LANES] = e[CHUNK:]

    def value_phase(g):
        base = g * 4 * CHUNK
        for p in range(ATTN_HEADS // 2):
            v_grp = jnp.concatenate([v_buf[0, base:base + GROUP_KEYS, ext_cols(p)],
                                     v_buf[1, base:base + GROUP_KEYS, ext_cols(p)]], axis=0)
            acc = _dot(p_buf[g % 2][staged + p], v_grp)
            attn_buf[base:base + 4 * CHUNK, pair_cols(2 * p)] = (acc[:, :PAIR] / acc[:, PAIR:]).astype(jnp.bfloat16)

    mem_steps = 2
    mem_per_step = n_mem // mem_steps
    assert mem_steps + n_merge <= n_cp + 2
    for step in range(n_cp + 2):
        mem_blocks = range(step * mem_per_step, (step + 1) * mem_per_step) if step < mem_steps else ()
        for r in mem_blocks:
            mem_score(r)
        if step < n_cp:
            score_phase(step)
        if 0 <= step - mem_steps < n_merge:
            nb = step - mem_steps
            part_buf[:, merge_cols(nb)] = (gated(0, pool_buf, w_up_pool_ref, nb)
                                           + gated(2, mem_buf, w_up_mem_ref, nb))
        if 0 <= step - 1 < n_cp:
            softmax_phase(step - 1)
        if step >= 3 and (step - 3) % 2 == 0 and (step - 3) // 2 < n_grp:
            value_phase((step - 3) // 2)
        for r in mem_blocks:
            mem_value(r)

    k_buf[CHUNK:KV_CARRY, :] = k_buf[ts + CHUNK:ts + KV_CARRY, :]
    for p in range(ATTN_HEADS // 2):
        vcols = slice(p * 2 * PAIR, p * 2 * PAIR + PAIR)
        v_buf[:, CHUNK:KV_CARRY, vcols] = v_buf[:, ts + CHUNK:ts + KV_CARRY, vcols]

    for nb in range(n_merge):
        merged = part_buf[:, merge_cols(nb)] + gated(1, attn_buf, w_up_attn_ref, nb)
        merged_buf[:, merge_cols(nb)] = merged.astype(jnp.bfloat16)
    o_ref[...] = x_ref[...] + _dot(merged_buf[...], w_out_ref[...])


def _ffn_kernel(x_ref, g_ref, w_up_ref, conv_w_ref, conv_b_ref, w_down_ref, g_final_ref, o_ref,
                h_buf, a_buf, tail_buf, hid_buf):
    sub = a_buf.shape[0] - CONV_PAD

    @pl.when(pl.program_id(0) == 0)
    def _init():
        tail_buf[...] = jnp.zeros_like(tail_buf)

    def conv_up(rows, cols):
        a = _dot(h_buf[rows, :], w_up_ref[:, cols])
        a_buf[0:CONV_PAD, :] = tail_buf[:, cols]
        a_buf[CONV_PAD:CONV_PAD + sub, :] = a
        tail_buf[:, cols] = a_buf[sub:sub + CONV_PAD, :]
        out = conv_b_ref[:, cols] + a * conv_w_ref[CONV_WIDTH - 1:CONV_WIDTH, cols]
        for t in range(CONV_WIDTH - 1):
            back = CONV_WIDTH - 1 - t
            out = out + a_buf[CONV_PAD - back:CONV_PAD - back + sub, :] * conv_w_ref[t:t + 1, cols]
        return out

    for r0 in range(0, x_ref.shape[0], sub):
        rows = slice(r0, r0 + sub)
        h_buf[rows, :] = _rms_norm(x_ref[rows, :], g_ref[...]).astype(jnp.bfloat16)
        for jb in range(D_FF // FF_BLOCK):
            gate = conv_up(rows, slice(jb * FF_BLOCK, (jb + 1) * FF_BLOCK))
            val = conv_up(rows, slice(D_FF + jb * FF_BLOCK, D_FF + (jb + 1) * FF_BLOCK))
            act = 0.5 * gate * (1.0 + lax.erf(gate * (2.0 ** -0.5)))
            hid_buf[rows, jb * FF_BLOCK:(jb + 1) * FF_BLOCK] = (act * val).astype(jnp.bfloat16)
        y = x_ref[rows, :] + _dot(hid_buf[rows, :], w_down_ref[...])
        o_ref[rows, :] = _rms_norm(y, g_final_ref[...])


def _resident(shape):
    zeros = (0,) * len(shape)
    return pl.BlockSpec(shape, lambda i: zeros, pipeline_mode=pl.Buffered(1))


def _rel_vector(rel_bias):
    assert CHUNK - 1 <= MAX_REL <= WIN_KEYS - CHUNK - 1
    n_clipped = WIN_KEYS - CHUNK - MAX_REL
    far = jnp.broadcast_to(rel_bias[:, 2 * MAX_REL:], (ATTN_HEADS, n_clipped))
    near = rel_bias[:, MAX_REL - (CHUNK - 1):2 * MAX_REL][:, ::-1]
    unused = jnp.zeros((ATTN_HEADS, 1), rel_bias.dtype)
    return jnp.concatenate([unused, far, near], axis=1).astype(jnp.float32)


def _block_diag(w):
    g, c, _ = w.shape
    eye = jnp.eye(g, dtype=w.dtype)
    return (eye[:, None, :, None] * w[:, :, None, :]).reshape(g * c, g * c)


def kernel(x, mem, norm_mix_g, norm_mem_g, w_in, b_gate, w_pool, pool_scale, rel_bias, w_mem_kv,
           w_up_pool, w_up_attn, w_up_mem, w_out, norm_ffn_g, w_ffn_up, conv_w, conv_b, w_ffn_down,
           norm_final_g):
    batch, seq, d = x.shape
    assert batch == 1 and d == D_MODEL and seq % SEQ_TILE == 0
    assert norm_mix_g.shape[0] == 1, "single layer"
    bf16 = jnp.bfloat16
    f32 = jnp.float32
    ts = SEQ_TILE
    n_tiles = seq // ts
    mem_len = mem.shape[1]
    row = lambda a: a.reshape(1, -1).astype(f32)
    params = pltpu.CompilerParams(dimension_semantics=("arbitrary",), vmem_limit_bytes=VMEM_LIMIT)

    km, vm = pl.pallas_call(
        _mem_kv_kernel,
        out_shape=(jax.ShapeDtypeStruct((mem_len, MEM_WIDTH), bf16),
                   jax.ShapeDtypeStruct((MEM_HEADS // 2, 2 * mem_len, 2 * PAIR), bf16)),
        name="mem_kv",
    )(mem[0], row(norm_mem_g[0]), w_mem_kv[0].astype(bf16))

    tile_spec = pl.BlockSpec((ts, D_MODEL), lambda i: (i, 0))
    mixer_inputs = (
        row(norm_mix_g[0]), w_in[0].astype(bf16), row(b_gate[0]), _block_diag(w_pool[0]).astype(bf16),
        row(pool_scale[0]), _rel_vector(rel_bias[0]), km, vm,
        w_up_pool[0].astype(bf16), w_up_attn[0].astype(bf16), w_up_mem[0].astype(bf16), w_out[0].astype(bf16),
    )
    x1 = pl.pallas_call(
        _mixer_kernel,
        grid=(n_tiles,),
        in_specs=[pl.BlockSpec(memory_space=pltpu.SMEM), tile_spec] + [_resident(a.shape) for a in mixer_inputs],
        out_specs=tile_spec,
        out_shape=jax.ShapeDtypeStruct((seq, D_MODEL), f32),
        scratch_shapes=[
            pltpu.VMEM((ts, D_MODEL), bf16),
            pltpu.VMEM((POOL_TAIL + ts, POOL_WIDTH), f32),
            pltpu.VMEM((2, ts, ATTN_WIDTH), bf16),
            pltpu.VMEM((KV_CARRY + ts + KV_TAIL, ATTN_WIDTH), bf16),
            pltpu.VMEM((2, KV_CARRY + ts + KV_TAIL, 2 * ATTN_WIDTH), bf16),
            pltpu.VMEM((2, ts, MEM_WIDTH), bf16),
            pltpu.VMEM((ts, POOL_WIDTH), bf16),
            pltpu.VMEM((ts, ATTN_WIDTH), bf16),
            pltpu.VMEM((ts, MEM_WIDTH), bf16),
            pltpu.VMEM((ts, D_MODEL), bf16),
            pltpu.VMEM((ts, D_MODEL), f32),
            pltpu.VMEM((ts // MEM_ROWS, MEM_HEADS // 2, MEM_ROWS, 2 * mem_len), bf16),
            pltpu.VMEM((2, ATTN_HEADS // 2, 2 * CHUNK, WIN_KEYS), f32),
            pltpu.VMEM((2, ATTN_HEADS // 2, 2 * CHUNK, WIN_KEYS), f32),
            pltpu.VMEM((2, ATTN_HEADS // 2, 2 * CHUNK, WIN_KEYS), f32),
            pltpu.VMEM((2, ATTN_HEADS // 2, 2 * CHUNK, LANES), f32),
            pltpu.VMEM((2, ATTN_HEADS // 2, 2 * CHUNK, LANES), f32),
            pltpu.VMEM((ATTN_HEADS // 2, 4 * CHUNK, 2 * GROUP_KEYS), bf16),
            pltpu.VMEM((ATTN_HEADS // 2, 4 * CHUNK, 2 * GROUP_KEYS), bf16),
        ],
        compiler_params=params,
        name="mixer",
    )(jnp.zeros((1,), jnp.int32), x[0], *mixer_inputs)

    ffn_inputs = (
        row(norm_ffn_g[0]), w_ffn_up[0].astype(bf16), conv_w[0].astype(f32), row(conv_b[0]),
        w_ffn_down[0].astype(bf16), row(norm_final_g),
    )
    tf = FFN_TILE
    ffn_tile_spec = pl.BlockSpec((tf, D_MODEL), lambda i: (i, 0))
    out = pl.pallas_call(
        _ffn_kernel,
        grid=(seq // tf,),
        in_specs=[ffn_tile_spec] + [_resident(a.shape) for a in ffn_inputs],
        out_specs=ffn_tile_spec,
        out_shape=jax.ShapeDtypeStruct((seq, D_MODEL), x.dtype),
        scratch_shapes=[
            pltpu.VMEM((tf, D_MODEL), bf16),
            pltpu.VMEM((CONV_PAD + FFN_SUB, FF_BLOCK), f32),
            pltpu.VMEM((CONV_PAD, 2 * D_FF), f32),
            pltpu.VMEM((tf, D_FF), bf16),
        ],
        compiler_params=params,
        name="ffn",
    )(x1, *ffn_inputs)
    return out[None]
```

```python
import functools

import jax
import jax.numpy as jnp
from jax import lax
from jax.experimental import pallas as pl
from jax.experimental.pallas import tpu as pltpu

D_MODEL = 1024
CHUNK = 64
HEAD_DIM = 64
POOL_WINDOWS = (2, 4, 8, 16)
POOL_GROUP_DIM = 64
POOL_WIDTH = len(POOL_WINDOWS) * POOL_GROUP_DIM
ATTN_HEADS = 8
ATTN_WIDTH = ATTN_HEADS * HEAD_DIM
BAND_CHUNKS = 9
BAND_KEYS = BAND_CHUNKS * CHUNK
MAX_REL = 128
MEM_HEADS = 4
MEM_WIDTH = MEM_HEADS * HEAD_DIM
N_BRANCH = 3
D_FF = 2816
CONV_WIDTH = 3
RMS_EPS = 1e-6
NEG_INF = -1e30

O_POOL = 0
O_Q = O_POOL + POOL_WIDTH
O_K = O_Q + ATTN_WIDTH
O_V = O_K + ATTN_WIDTH
O_QM = O_V + ATTN_WIDTH
O_GATE = O_QM + MEM_WIDTH

LANES = 128
SUBLANES = 8
BF16_ROWS = 2 * SUBLANES
PAIR = 2 * HEAD_DIM
POOL_TAIL = 16
WIN_KEYS = BAND_KEYS + CHUNK
KV_CARRY = WIN_KEYS - CHUNK
UNION_KEYS = WIN_KEYS + 2 * CHUNK
KV_TAIL = CHUNK
SEQ_TILE = 512
FFN_TILE = 1024
FFN_SUB = 1024
MEM_ROWS = 128
FF_BLOCK = 256
CONV_PAD = SUBLANES
VMEM_LIMIT = 60 * 1024 * 1024


def _rms_norm(x, g):
    y = x * lax.rsqrt(jnp.mean(x * x, axis=-1, keepdims=True) + RMS_EPS)
    return y * g


def _dot(a, b):
    return jnp.dot(a, b, preferred_element_type=jnp.float32)


def _dot_nt(a, b):
    return lax.dot_general(a, b, (((1,), (1,)), ((), ())), preferred_element_type=jnp.float32)


def _mem_kv_kernel(mem_ref, g_ref, w_ref, k_ref, v_ref):
    mem_len = mem_ref.shape[0]
    mem_n = _rms_norm(mem_ref[...], g_ref[...]).astype(jnp.bfloat16)
    kv = _dot(mem_n, w_ref[...])
    k_ref[...] = kv[:, :MEM_WIDTH].astype(jnp.bfloat16)
    for p in range(MEM_HEADS // 2):
        v_ref[p, :, 0:PAIR] = kv[:, MEM_WIDTH + p * PAIR:MEM_WIDTH + (p + 1) * PAIR].astype(jnp.bfloat16)
        v_ref[p, :, PAIR:2 * PAIR] = jnp.ones((mem_len, PAIR), jnp.bfloat16)


def _mixer_kernel(zero_ref, x_ref, g_ref, w_in_ref, b_gate_ref, w_pool_ref, pool_scale_ref, rel_vec_ref,
                  km_ref, vm_ref, w_up_pool_ref, w_up_attn_ref, w_up_mem_ref, w_out_ref,
                  w_ffn_up_ref, w_ffn_down_ref,
                  o_ref, w_ffn_up_bf16_ref, w_ffn_down_bf16_ref,
                  h_buf, u_buf, q_buf, k_buf, v_buf, qm_buf, pool_buf, attn_buf, mem_buf, merged_buf,
                  part_buf, pm_buf, bias_buf, s_buf0, s_buf1, m_buf0, m_buf1, p_buf0, p_buf1):
    s_buf, m_buf, p_buf = (s_buf0, s_buf1), (m_buf0, m_buf1), (p_buf0, p_buf1)
    ts = x_ref.shape[0]
    tile = pl.program_id(0)

    @pl.when(tile == 0)
    def _init():
        u_buf[0:POOL_TAIL, :] = jnp.zeros((POOL_TAIL, POOL_WIDTH), jnp.float32)
        k_buf[...] = jnp.zeros(k_buf.shape, jnp.bfloat16)
        ext_lane = lax.broadcasted_iota(jnp.int32, v_buf.shape, 1) % (2 * PAIR)
        v_buf[...] = jnp.where(ext_lane >= PAIR, 1.0, 0.0).astype(jnp.bfloat16)
        for buf in p_buf:
            buf[...] = jnp.zeros(buf.shape, jnp.bfloat16)
        win_lane = lax.broadcasted_iota(jnp.int32, (1, WIN_KEYS), 1)
        for h in range(ATTN_HEADS):
            vec = rel_vec_ref[h:h + 1, :]
            for qi in range(CHUNK):
                r = (h % 2) * CHUNK + qi
                even_row = pltpu.roll(vec, qi, axis=1) if qi else vec
                bias_buf[0, h // 2, r:r + 1, :] = jnp.where(win_lane < CHUNK, NEG_INF, even_row)
                odd_row = pltpu.roll(vec, qi + BAND_KEYS, axis=1)
                bias_buf[1, h // 2, r:r + 1, :] = jnp.where(win_lane >= BAND_KEYS, NEG_INF, odd_row)

    w_ffn_up_bf16_ref[...] = w_ffn_up_ref[...].astype(jnp.bfloat16)
    w_ffn_down_bf16_ref[...] = w_ffn_down_ref[...].astype(jnp.bfloat16)

    h_buf[...] = _rms_norm(x_ref[...], g_ref[...]).astype(jnp.bfloat16)

    u_buf[POOL_TAIL:POOL_TAIL + ts, :] = _dot(h_buf[...], w_in_ref[:, O_POOL:O_Q])
    def split_heads(q, even_ref, odd_ref):
        odd = (lax.broadcasted_iota(jnp.int32, q.shape, 1) // HEAD_DIM) % 2 == 1
        q = q * (HEAD_DIM ** -0.5)
        even_ref[...] = jnp.where(odd, 0.0, q).astype(jnp.bfloat16)
        odd_ref[...] = jnp.where(odd, q, 0.0).astype(jnp.bfloat16)

    split_heads(_dot(h_buf[...], w_in_ref[:, O_Q:O_K]), q_buf.at[0], q_buf.at[1])
    k_buf[KV_CARRY:KV_CARRY + ts, :] = _dot(h_buf[...], w_in_ref[:, O_K:O_V]).astype(jnp.bfloat16)
    v = _dot(h_buf[...], w_in_ref[:, O_V:O_QM]).astype(jnp.bfloat16)
    for p in range(ATTN_HEADS // 2):
        v_buf[KV_CARRY:KV_CARRY + ts, p * 2 * PAIR:p * 2 * PAIR + PAIR] = v[:, p * PAIR:(p + 1) * PAIR]
    split_heads(_dot(h_buf[...], w_in_ref[:, O_QM:O_GATE]), qm_buf.at[0], qm_buf.at[1])

    u = u_buf[POOL_TAIL:POOL_TAIL + ts, :]
    lane_group = lax.broadcasted_iota(jnp.int32, (ts, POOL_WIDTH), 1) // POOL_GROUP_DIM
    run = u
    win_sum = jnp.zeros_like(u)
    for j in range(1, POOL_TAIL + 1):
        if j in POOL_WINDOWS:
            win_sum = jnp.where(lane_group == POOL_WINDOWS.index(j), run, win_sum)
        if j < POOL_TAIL:
            run = run + u_buf[POOL_TAIL - j:POOL_TAIL - j + ts, :]
    window = jnp.left_shift(2, lane_group)
    pos1 = tile * ts + lax.broadcasted_iota(jnp.int32, (ts, POOL_WIDTH), 0) + 1
    cnt = jnp.minimum(pos1, window).astype(jnp.float32)
    pooled = (win_sum / cnt - u).astype(jnp.bfloat16)
    pool_buf[...] = (_dot(pooled, w_pool_ref[...]) * pool_scale_ref[...]).astype(jnp.bfloat16)
    u_buf[0:POOL_TAIL, :] = u_buf[ts:ts + POOL_TAIL, :]

    def pair_cols(h):
        return slice((h // 2) * PAIR, (h // 2 + 1) * PAIR)

    staged = zero_ref[0]

    def pair_output(acc):
        half = acc.shape[0] // 2
        out = acc[:, :PAIR] / acc[:, PAIR:]
        lane_is_odd = lax.broadcasted_iota(jnp.int32, (half, PAIR), 1) >= HEAD_DIM
        return jnp.where(lane_is_odd, out[half:], out[:half]).astype(jnp.bfloat16)

    mem_len = km_ref.shape[0]
    n_mem = ts // MEM_ROWS

    def mem_score(r):
        r0 = r * MEM_ROWS
        for h in range(MEM_HEADS):
            s = _dot_nt(qm_buf[h % 2, r0:r0 + MEM_ROWS, pair_cols(h)], km_ref[:, pair_cols(h)])
            e = jnp.exp(s - jnp.max(s, axis=-1, keepdims=True))
            pm_buf[r, h // 2, (h % 2) * MEM_ROWS:(h % 2 + 1) * MEM_ROWS, :] = e.astype(jnp.bfloat16)

    def mem_value(r):
        r0 = r * MEM_ROWS
        for p in range(MEM_HEADS // 2):
            acc = _dot(pm_buf[staged + r, p], vm_ref[p])
            mem_buf[r0:r0 + MEM_ROWS, pair_cols(2 * p)] = pair_output(acc)

    merge_blk = 2 * LANES
    n_merge = D_MODEL // merge_blk

    def merge_cols(nb):
        return slice(nb * merge_blk, (nb + 1) * merge_blk)

    def gated(b, branch_buf, w_up_ref, nb):
        lo = b * D_MODEL + nb * merge_blk
        logits = _dot(h_buf[...], w_in_ref[:, O_GATE + lo:O_GATE + lo + merge_blk]) + b_gate_ref[:, lo:lo + merge_blk]
        return jax.nn.sigmoid(logits) * _dot(branch_buf[...], w_up_ref[:, merge_cols(nb)])

    win_col = lax.broadcasted_iota(jnp.int32, (1, WIN_KEYS), 1)
    pad_rows = jnp.where(tile == 0, KV_CARRY, 0)

    def ext_cols(p):
        return slice(p * 2 * PAIR, (p + 1) * 2 * PAIR)

    n_k = WIN_KEYS // LANES
    n_cp = ts // (2 * CHUNK)

    def score_phase(cp):
        base = cp * 2 * CHUNK
        slot = cp % 2
        for j in range(2):
            q0 = base + j * CHUNK
            w0 = base + j * 2 * CHUNK
            pen = jnp.where(win_col + w0 < pad_rows, NEG_INF, 0.0)
            for p in range(ATTN_HEADS // 2):
                q_pair = jnp.concatenate([q_buf[0, q0:q0 + CHUNK, pair_cols(2 * p)],
                                          q_buf[1, q0:q0 + CHUNK, pair_cols(2 * p)]], axis=0)
                s = _dot_nt(q_pair, k_buf[w0:w0 + WIN_KEYS, pair_cols(2 * p)]) + bias_buf[j, p] + pen
                s_buf[slot][j, p] = s
                part = s[:, 0:LANES]
                for k in range(1, n_k):
                    part = jnp.maximum(part, s[:, k * LANES:(k + 1) * LANES])
                m_buf[slot][j, p] = jnp.broadcast_to(jnp.max(part, axis=-1, keepdims=True), (2 * CHUNK, LANES))

    def softmax_phase(cp):
        slot = cp % 2
        for j in range(2):
            for p in range(ATTN_HEADS // 2):
                m = m_buf[slot][staged + j, p]
                for k in range(n_k):
                    e = jnp.exp(s_buf[slot][staged + j, p, :, k * LANES:(k + 1) * LANES] - m).astype(jnp.bfloat16)
                    lo = j * 2 * CHUNK + k * LANES
                    p_buf[slot][p, j * CHUNK:(j + 1) * CHUNK, lo:lo + LANES] = e[:CHUNK]
                    p_buf[slot][p, (2 + j) * CHUNK:(3 + j) * CHUNK, lo:lo + LANES] = e[CHUNK:]

    def value_phase(cp):
        base = cp * 2 * CHUNK
        slot = cp % 2
        for p in range(ATTN_HEADS // 2):
            acc = _dot(p_buf[slot][staged + p], v_buf[base:base + UNION_KEYS, ext_cols(p)])
            attn_buf[base:base + 2 * CHUNK, pair_cols(2 * p)] = pair_output(acc)

    mem_steps = 2
    mem_per_step = n_mem // mem_steps
    assert mem_steps + n_merge <= n_cp + 2
    for step in range(n_cp + 2):
        mem_blocks = range(step * mem_per_step, (step + 1) * mem_per_step) if step < mem_steps else ()
        for r in mem_blocks:
            mem_score(r)
        if step < n_cp:
            score_phase(step)
        if 0 <= step - mem_steps < n_merge:
            nb = step - mem_steps
            part_buf[:, merge_cols(nb)] = (gated(0, pool_buf, w_up_pool_ref, nb)
                                           + gated(2, mem_buf, w_up_mem_ref, nb))
        if 0 <= step - 1 < n_cp:
            softmax_phase(step - 1)
        if 0 <= step - 2 < n_cp:
            value_phase(step - 2)
        for r in mem_blocks:
            mem_value(r)

    k_buf[CHUNK:KV_CARRY, :] = k_buf[ts + CHUNK:ts + KV_CARRY, :]
    for p in range(ATTN_HEADS // 2):
        vcols = slice(p * 2 * PAIR, p * 2 * PAIR + PAIR)
        v_buf[CHUNK:KV_CARRY, vcols] = v_buf[ts + CHUNK:ts + KV_CARRY, vcols]

    for nb in range(n_merge):
        merged = part_buf[:, merge_cols(nb)] + gated(1, attn_buf, w_up_attn_ref, nb)
        merged_buf[:, merge_cols(nb)] = merged.astype(jnp.bfloat16)
    o_ref[...] = x_ref[...] + _dot(merged_buf[...], w_out_ref[...])


def _ffn_kernel(x_ref, g_ref, w_up_ref, conv_w_ref, conv_b_ref, w_down_ref, g_final_ref, o_ref,
                h_buf, a_buf, tail_buf, hid_buf):
    sub = a_buf.shape[0] - CONV_PAD

    @pl.when(pl.program_id(0) == 0)
    def _init():
        tail_buf[...] = jnp.zeros_like(tail_buf)

    def conv_up(rows, cols):
        a = _dot(h_buf[rows, :], w_up_ref[:, cols])
        a_buf[0:CONV_PAD, :] = tail_buf[:, cols]
        a_buf[CONV_PAD:CONV_PAD + sub, :] = a
        tail_buf[:, cols] = a_buf[sub:sub + CONV_PAD, :]
        out = conv_b_ref[:, cols] + a * conv_w_ref[CONV_WIDTH - 1:CONV_WIDTH, cols]
        for t in range(CONV_WIDTH - 1):
            back = CONV_WIDTH - 1 - t
            out = out + a_buf[CONV_PAD - back:CONV_PAD - back + sub, :] * conv_w_ref[t:t + 1, cols]
        return out

    for r0 in range(0, x_ref.shape[0], sub):
        rows = slice(r0, r0 + sub)
        h_buf[rows, :] = _rms_norm(x_ref[rows, :], g_ref[...]).astype(jnp.bfloat16)
        for jb in range(D_FF // FF_BLOCK):
            gate = conv_up(rows, slice(jb * FF_BLOCK, (jb + 1) * FF_BLOCK))
            val = conv_up(rows, slice(D_FF + jb * FF_BLOCK, D_FF + (jb + 1) * FF_BLOCK))
            act = 0.5 * gate * (1.0 + lax.erf(gate * (2.0 ** -0.5)))
            hid_buf[rows, jb * FF_BLOCK:(jb + 1) * FF_BLOCK] = (act * val).astype(jnp.bfloat16)
        for q0 in range(r0, r0 + sub, sub // 4):
            quarter = slice(q0, q0 + sub // 4)
            y = x_ref[quarter, :] + _dot(hid_buf[quarter, :], w_down_ref[...])
            o_ref[quarter, :] = _rms_norm(y, g_final_ref[...])


def _resident(shape):
    zeros = (0,) * len(shape)
    return pl.BlockSpec(shape, lambda i: zeros, pipeline_mode=pl.Buffered(1))


def _rel_vector(rel_bias):
    assert CHUNK - 1 <= MAX_REL <= WIN_KEYS - CHUNK - 1
    n_clipped = WIN_KEYS - CHUNK - MAX_REL
    far = jnp.broadcast_to(rel_bias[:, 2 * MAX_REL:], (ATTN_HEADS, n_clipped))
    near = rel_bias[:, MAX_REL - (CHUNK - 1):2 * MAX_REL][:, ::-1]
    unused = jnp.zeros((ATTN_HEADS, 1), rel_bias.dtype)
    return jnp.concatenate([unused, far, near], axis=1).astype(jnp.float32)


def _block_diag(w):
    g, c, _ = w.shape
    eye = jnp.eye(g, dtype=w.dtype)
    return (eye[:, None, :, None] * w[:, :, None, :]).reshape(g * c, g * c)


def kernel(x, mem, norm_mix_g, norm_mem_g, w_in, b_gate, w_pool, pool_scale, rel_bias, w_mem_kv,
           w_up_pool, w_up_attn, w_up_mem, w_out, norm_ffn_g, w_ffn_up, conv_w, conv_b, w_ffn_down,
           norm_final_g):
    batch, seq, d = x.shape
    assert batch == 1 and d == D_MODEL and seq % SEQ_TILE == 0
    assert norm_mix_g.shape[0] == 1, "single layer"
    bf16 = jnp.bfloat16
    f32 = jnp.float32
    ts = SEQ_TILE
    n_tiles = seq // ts
    mem_len = mem.shape[1]
    row = lambda a: a.reshape(1, -1).astype(f32)
    params = pltpu.CompilerParams(dimension_semantics=("arbitrary",), vmem_limit_bytes=VMEM_LIMIT)

    km, vm = pl.pallas_call(
        _mem_kv_kernel,
        out_shape=(jax.ShapeDtypeStruct((mem_len, MEM_WIDTH), bf16),
                   jax.ShapeDtypeStruct((MEM_HEADS // 2, mem_len, 2 * PAIR), bf16)),
        name="mem_kv",
    )(mem[0], row(norm_mem_g[0]), w_mem_kv[0].astype(bf16))

    tile_spec = pl.BlockSpec((ts, D_MODEL), lambda i: (i, 0))
    mixer_inputs = (
        row(norm_mix_g[0]), w_in[0].astype(bf16), row(b_gate[0]), _block_diag(w_pool[0]).astype(bf16),
        row(pool_scale[0]), _rel_vector(rel_bias[0]), km, vm,
        w_up_pool[0].astype(bf16), w_up_attn[0].astype(bf16), w_up_mem[0].astype(bf16), w_out[0].astype(bf16),
    )
    ffn_w = (w_ffn_up[0], w_ffn_down[0])
    slab_steps = (1, 2)
    slab_specs = [pl.BlockSpec((w.shape[0] * k // n_tiles, w.shape[1]), functools.partial(lambda i, k: (i // k, 0), k=k))
                  for w, k in zip(ffn_w, slab_steps)]
    assert all((w.shape[0] * k) % (n_tiles * BF16_ROWS) == 0 for w, k in zip(ffn_w, slab_steps))
    x1, w_up_bf16, w_down_bf16 = pl.pallas_call(
        _mixer_kernel,
        grid=(n_tiles,),
        in_specs=[pl.BlockSpec(memory_space=pltpu.SMEM), tile_spec] + [_resident(a.shape) for a in mixer_inputs] + slab_specs,
        out_specs=[tile_spec] + slab_specs,
        out_shape=[jax.ShapeDtypeStruct((seq, D_MODEL), f32)] + [jax.ShapeDtypeStruct(w.shape, bf16) for w in ffn_w],
        scratch_shapes=[
            pltpu.VMEM((ts, D_MODEL), bf16),
            pltpu.VMEM((POOL_TAIL + ts, POOL_WIDTH), f32),
            pltpu.VMEM((2, ts, ATTN_WIDTH), bf16),
            pltpu.VMEM((KV_CARRY + ts + KV_TAIL, ATTN_WIDTH), bf16),
            pltpu.VMEM((KV_CARRY + ts + KV_TAIL, 2 * ATTN_WIDTH), bf16),
            pltpu.VMEM((2, ts, MEM_WIDTH), bf16),
            pltpu.VMEM((ts, POOL_WIDTH), bf16),
            pltpu.VMEM((ts, ATTN_WIDTH), bf16),
            pltpu.VMEM((ts, MEM_WIDTH), bf16),
            pltpu.VMEM((ts, D_MODEL), bf16),
            pltpu.VMEM((ts, D_MODEL), f32),
            pltpu.VMEM((ts // MEM_ROWS, MEM_HEADS // 2, 2 * MEM_ROWS, mem_len), bf16),
            pltpu.VMEM((2, ATTN_HEADS // 2, 2 * CHUNK, WIN_KEYS), f32),
            pltpu.VMEM((2, ATTN_HEADS // 2, 2 * CHUNK, WIN_KEYS), f32),
            pltpu.VMEM((2, ATTN_HEADS // 2, 2 * CHUNK, WIN_KEYS), f32),
            pltpu.VMEM((2, ATTN_HEADS // 2, 2 * CHUNK, LANES), f32),
            pltpu.VMEM((2, ATTN_HEADS // 2, 2 * CHUNK, LANES), f32),
            pltpu.VMEM((ATTN_HEADS // 2, 4 * CHUNK, UNION_KEYS), bf16),
            pltpu.VMEM((ATTN_HEADS // 2, 4 * CHUNK, UNION_KEYS), bf16),
        ],
        compiler_params=params,
        name="mixer",
    )(jnp.zeros((1,), jnp.int32), x[0], *mixer_inputs, *ffn_w)

    ffn_inputs = (
        row(norm_ffn_g[0]), w_up_bf16, conv_w[0].astype(f32), row(conv_b[0]),
        w_down_bf16, row(norm_final_g),
    )
    tf = FFN_TILE
    ffn_tile_spec = pl.BlockSpec((tf, D_MODEL), lambda i: (i, 0))
    out = pl.pallas_call(
        _ffn_kernel,
        grid=(seq // tf,),
        in_specs=[ffn_tile_spec] + [_resident(a.shape) for a in ffn_inputs],
        out_specs=ffn_tile_spec,
        out_shape=jax.ShapeDtypeStruct((seq, D_MODEL), x.dtype),
        scratch_shapes=[
            pltpu.VMEM((tf, D_MODEL), bf16),
            pltpu.VMEM((CONV_PAD + FFN_SUB, FF_BLOCK), f32),
            pltpu.VMEM((CONV_PAD, 2 * D_FF), f32),
            pltpu.VMEM((tf, D_FF), bf16),
        ],
        compiler_params=params,
        name="ffn",
    )(x1, *ffn_inputs)
    return out[None]
```

```python
import functools

import jax
import jax.numpy as jnp
from jax import lax
from jax.experimental import pallas as pl
from jax.experimental.pallas import tpu as pltpu

D_MODEL = 1024
CHUNK = 64
HEAD_DIM = 64
POOL_WINDOWS = (2, 4, 8, 16)
POOL_GROUP_DIM = 64
POOL_WIDTH = len(POOL_WINDOWS) * POOL_GROUP_DIM
ATTN_HEADS = 8
ATTN_WIDTH = ATTN_HEADS * HEAD_DIM
BAND_CHUNKS = 9
BAND_KEYS = BAND_CHUNKS * CHUNK
MAX_REL = 128
MEM_HEADS = 4
MEM_WIDTH = MEM_HEADS * HEAD_DIM
N_BRANCH = 3
D_FF = 2816
CONV_WIDTH = 3
RMS_EPS = 1e-6
NEG_INF = -1e30

O_POOL = 0
O_Q = O_POOL + POOL_WIDTH
O_K = O_Q + ATTN_WIDTH
O_V = O_K + ATTN_WIDTH
O_QM = O_V + ATTN_WIDTH
O_GATE = O_QM + MEM_WIDTH

LANES = 128
SUBLANES = 8
BF16_ROWS = 2 * SUBLANES
PAIR = 2 * HEAD_DIM
POOL_TAIL = 16
WIN_KEYS = BAND_KEYS + CHUNK
KV_CARRY = BAND_KEYS - CHUNK
SEQ_TILE = 512
FFN_TILE = 1024
FFN_SUB = 1024
MEM_ROWS = 128
PREP_STEPS = 8
FF_BLOCK = 256
CONV_PAD = SUBLANES
VMEM_LIMIT = 60 * 1024 * 1024


def _rms_norm(x, g):
    y = x * lax.rsqrt(jnp.mean(x * x, axis=-1, keepdims=True) + RMS_EPS)
    return y * g


def _dot(a, b):
    return jnp.dot(a, b, preferred_element_type=jnp.float32)


def _dot_nt(a, b):
    return lax.dot_general(a, b, (((1,), (1,)), ((), ())), preferred_element_type=jnp.float32)


def _prep_kernel(mem_ref, g_ref, w_kv_ref, w_in_ref, *refs):
    n_small = (len(refs) - 3) // 2
    small_in = refs[:n_small]
    k_ref, v_ref, w_in_bf16_ref = refs[n_small:n_small + 3]
    small_out = refs[n_small + 3:]
    w_in_bf16_ref[...] = w_in_ref[...].astype(jnp.bfloat16)

    @pl.when(pl.program_id(0) == 0)
    def _first():
        for src, dst in zip(small_in, small_out):
            dst[...] = src[...].astype(jnp.bfloat16)
        mem_len = mem_ref.shape[0]
        mem_n = _rms_norm(mem_ref[...], g_ref[...]).astype(jnp.bfloat16)
        kv = _dot(mem_n, w_kv_ref[...].astype(jnp.bfloat16))
        k_ref[...] = kv[:, :MEM_WIDTH].astype(jnp.bfloat16)
        for p in range(MEM_HEADS // 2):
            v_ref[p, :, 0:PAIR] = kv[:, MEM_WIDTH + p * PAIR:MEM_WIDTH + (p + 1) * PAIR].astype(jnp.bfloat16)
            v_ref[p, :, PAIR:2 * PAIR] = jnp.ones((mem_len, PAIR), jnp.bfloat16)


def _mixer_kernel(zero_ref, x_ref, g_ref, w_in_ref, b_gate_ref, w_pool_ref, pool_scale_ref, rel_vec_ref,
                  km_ref, vm_ref, w_up_pool_ref, w_up_attn_ref, w_up_mem_ref, w_out_ref,
                  w_ffn_up_ref, w_ffn_down_ref,
                  o_ref, w_ffn_up_bf16_ref, w_ffn_down_bf16_ref,
                  h_buf, u_buf, q_buf, k_buf, v_buf, qm_buf, pool_buf, attn_buf, mem_buf, merged_buf,
                  part_buf, pm_buf, bias_buf, s_buf0, s_buf1, m_buf0, m_buf1, p_buf0, p_buf1):
    s_buf, m_buf, p_buf = (s_buf0, s_buf1), (m_buf0, m_buf1), (p_buf0, p_buf1)
    ts = x_ref.shape[0]
    tile = pl.program_id(0)

    @pl.when(tile == 0)
    def _init():
        u_buf[0:POOL_TAIL, :] = jnp.zeros((POOL_TAIL, POOL_WIDTH), jnp.float32)
        k_buf[0:KV_CARRY, :] = jnp.zeros((KV_CARRY, ATTN_WIDTH), jnp.bfloat16)
        ext_lane = lax.broadcasted_iota(jnp.int32, v_buf.shape, 1) % (2 * PAIR)
        v_buf[...] = jnp.where(ext_lane >= PAIR, 1.0, 0.0).astype(jnp.bfloat16)
        win_lane = lax.broadcasted_iota(jnp.int32, (1, WIN_KEYS), 1)
        for h in range(ATTN_HEADS):
            vec = rel_vec_ref[h:h + 1, :]
            for qi in range(CHUNK):
                r = (h % 2) * 2 * CHUNK + qi
                first = pltpu.roll(vec, qi + BAND_KEYS, axis=1)
                bias_buf[h // 2, r:r + 1, :] = jnp.where(win_lane >= BAND_KEYS, NEG_INF, first)
                second = pltpu.roll(vec, qi, axis=1) if qi else vec
                bias_buf[h // 2, r + CHUNK:r + CHUNK + 1, :] = jnp.where(win_lane < CHUNK, NEG_INF, second)

    w_ffn_up_bf16_ref[...] = w_ffn_up_ref[...].astype(jnp.bfloat16)
    w_ffn_down_bf16_ref[...] = w_ffn_down_ref[...].astype(jnp.bfloat16)

    h_buf[...] = _rms_norm(x_ref[...], g_ref[...]).astype(jnp.bfloat16)

    u_buf[POOL_TAIL:POOL_TAIL + ts, :] = _dot(h_buf[...], w_in_ref[:, O_POOL:O_Q])
    def split_heads(q, even_ref, odd_ref):
        odd = (lax.broadcasted_iota(jnp.int32, q.shape, 1) // HEAD_DIM) % 2 == 1
        q = q * (HEAD_DIM ** -0.5)
        even_ref[...] = jnp.where(odd, 0.0, q).astype(jnp.bfloat16)
        odd_ref[...] = jnp.where(odd, q, 0.0).astype(jnp.bfloat16)

    split_heads(_dot(h_buf[...], w_in_ref[:, O_Q:O_K]), q_buf.at[0], q_buf.at[1])
    k_buf[KV_CARRY:KV_CARRY + ts, :] = _dot(h_buf[...], w_in_ref[:, O_K:O_V]).astype(jnp.bfloat16)
    v = _dot(h_buf[...], w_in_ref[:, O_V:O_QM]).astype(jnp.bfloat16)
    for p in range(ATTN_HEADS // 2):
        v_buf[KV_CARRY:KV_CARRY + ts, p * 2 * PAIR:p * 2 * PAIR + PAIR] = v[:, p * PAIR:(p + 1) * PAIR]
    split_heads(_dot(h_buf[...], w_in_ref[:, O_QM:O_GATE]), qm_buf.at[0], qm_buf.at[1])

    u = u_buf[POOL_TAIL:POOL_TAIL + ts, :]
    lane_group = lax.broadcasted_iota(jnp.int32, (ts, POOL_WIDTH), 1) // POOL_GROUP_DIM
    run = u
    win_sum = jnp.zeros_like(u)
    for j in range(1, POOL_TAIL + 1):
        if j in POOL_WINDOWS:
            win_sum = jnp.where(lane_group == POOL_WINDOWS.index(j), run, win_sum)
        if j < POOL_TAIL:
            run = run + u_buf[POOL_TAIL - j:POOL_TAIL - j + ts, :]
    window = jnp.left_shift(2, lane_group)
    pos1 = tile * ts + lax.broadcasted_iota(jnp.int32, (ts, POOL_WIDTH), 0) + 1
    cnt = jnp.minimum(pos1, window).astype(jnp.float32)
    pooled = (win_sum / cnt - u).astype(jnp.bfloat16)
    pool_buf[...] = (_dot(pooled, w_pool_ref[...]) * pool_scale_ref[...]).astype(jnp.bfloat16)
    u_buf[0:POOL_TAIL, :] = u_buf[ts:ts + POOL_TAIL, :]

    def pair_cols(h):
        return slice((h // 2) * PAIR, (h // 2 + 1) * PAIR)

    staged = zero_ref[0]

    def pair_output(acc):
        half = acc.shape[0] // 2
        out = acc[:, :PAIR] / acc[:, PAIR:]
        lane_is_odd = lax.broadcasted_iota(jnp.int32, (half, PAIR), 1) >= HEAD_DIM
        return jnp.where(lane_is_odd, out[half:], out[:half]).astype(jnp.bfloat16)

    mem_len = km_ref.shape[0]
    n_mem = ts // MEM_ROWS

    def mem_score(r):
        r0 = r * MEM_ROWS
        for h in range(MEM_HEADS):
            s = _dot_nt(qm_buf[h % 2, r0:r0 + MEM_ROWS, pair_cols(h)], km_ref[:, pair_cols(h)])
            e = jnp.exp(s - jnp.max(s, axis=-1, keepdims=True))
            pm_buf[r, h // 2, (h % 2) * MEM_ROWS:(h % 2 + 1) * MEM_ROWS, :] = e.astype(jnp.bfloat16)

    def mem_value(r):
        r0 = r * MEM_ROWS
        for p in range(MEM_HEADS // 2):
            acc = _dot(pm_buf[staged + r, p], vm_ref[p])
            mem_buf[r0:r0 + MEM_ROWS, pair_cols(2 * p)] = pair_output(acc)

    merge_blk = 2 * LANES
    n_merge = D_MODEL // merge_blk

    def merge_cols(nb):
        return slice(nb * merge_blk, (nb + 1) * merge_blk)

    def gated(b, branch_buf, w_up_ref, nb):
        lo = b * D_MODEL + nb * merge_blk
        logits = _dot(h_buf[...], w_in_ref[:, O_GATE + lo:O_GATE + lo + merge_blk]) + b_gate_ref[:, lo:lo + merge_blk]
        return jax.nn.sigmoid(logits) * _dot(branch_buf[...], w_up_ref[:, merge_cols(nb)])

    win_col = lax.broadcasted_iota(jnp.int32, (1, WIN_KEYS), 1)
    pad_rows = jnp.where(tile == 0, KV_CARRY, 0)

    def ext_cols(p):
        return slice(p * 2 * PAIR, (p + 1) * 2 * PAIR)

    n_k = WIN_KEYS // LANES
    n_cp = ts // (2 * CHUNK)

    def score_phase(cp):
        base = cp * 2 * CHUNK
        slot = cp % 2
        pen = jnp.where(win_col + base < pad_rows, NEG_INF, 0.0)
        for p in range(ATTN_HEADS // 2):
            q_rows = jnp.concatenate([q_buf[0, base:base + 2 * CHUNK, pair_cols(2 * p)],
                                      q_buf[1, base:base + 2 * CHUNK, pair_cols(2 * p)]], axis=0)
            s = _dot_nt(q_rows, k_buf[base:base + WIN_KEYS, pair_cols(2 * p)]) + bias_buf[p] + pen
            s_buf[slot][p] = s
            part = s[:, 0:LANES]
            for k in range(1, n_k):
                part = jnp.maximum(part, s[:, k * LANES:(k + 1) * LANES])
            m_buf[slot][p] = jnp.broadcast_to(jnp.max(part, axis=-1, keepdims=True), (4 * CHUNK, LANES))

    def softmax_phase(cp):
        slot = cp % 2
        for p in range(ATTN_HEADS // 2):
            m = m_buf[slot][staged + p]
            for k in range(n_k):
                cols = slice(k * LANES, (k + 1) * LANES)
                p_buf[slot][p, :, cols] = jnp.exp(s_buf[slot][staged + p, :, cols] - m).astype(jnp.bfloat16)

    def value_phase(cp):
        base = cp * 2 * CHUNK
        slot = cp % 2
        for p in range(ATTN_HEADS // 2):
            acc = _dot(p_buf[slot][staged + p], v_buf[base:base + WIN_KEYS, ext_cols(p)])
            attn_buf[base:base + 2 * CHUNK, pair_cols(2 * p)] = pair_output(acc)

    mem_steps = 2
    mem_per_step = n_mem // mem_steps
    assert mem_steps + n_merge <= n_cp + 2
    for step in range(n_cp + 2):
        mem_blocks = range(step * mem_per_step, (step + 1) * mem_per_step) if step < mem_steps else ()
        for r in mem_blocks:
            mem_score(r)
        if step < n_cp:
            score_phase(step)
        if 0 <= step - mem_steps < n_merge:
            nb = step - mem_steps
            part_buf[:, merge_cols(nb)] = (gated(0, pool_buf, w_up_pool_ref, nb)
                                           + gated(2, mem_buf, w_up_mem_ref, nb))
        if 0 <= step - 1 < n_cp:
            softmax_phase(step - 1)
        if 0 <= step - 2 < n_cp:
            value_phase(step - 2)
        for r in mem_blocks:
            mem_value(r)

    k_buf[0:KV_CARRY, :] = k_buf[ts:ts + KV_CARRY, :]
    for p in range(ATTN_HEADS // 2):
        vcols = slice(p * 2 * PAIR, p * 2 * PAIR + PAIR)
        v_buf[0:KV_CARRY, vcols] = v_buf[ts:ts + KV_CARRY, vcols]

    for nb in range(n_merge):
        merged = part_buf[:, merge_cols(nb)] + gated(1, attn_buf, w_up_attn_ref, nb)
        merged_buf[:, merge_cols(nb)] = merged.astype(jnp.bfloat16)
    o_ref[...] = x_ref[...] + _dot(merged_buf[...], w_out_ref[...])


def _ffn_kernel(x_ref, g_ref, w_up_ref, conv_w_ref, conv_b_ref, w_down_ref, g_final_ref, o_ref,
                h_buf, a_buf, tail_buf, hid_buf):
    sub = a_buf.shape[0] - CONV_PAD

    @pl.when(pl.program_id(0) == 0)
    def _init():
        tail_buf[...] = jnp.zeros_like(tail_buf)

    def conv_up(rows, cols):
        a = _dot(h_buf[rows, :], w_up_ref[:, cols])
        a_buf[0:CONV_PAD, :] = tail_buf[:, cols]
        a_buf[CONV_PAD:CONV_PAD + sub, :] = a
        tail_buf[:, cols] = a_buf[sub:sub + CONV_PAD, :]
        out = conv_b_ref[:, cols] + a * conv_w_ref[CONV_WIDTH - 1:CONV_WIDTH, cols]
        for t in range(CONV_WIDTH - 1):
            back = CONV_WIDTH - 1 - t
            out = out + a_buf[CONV_PAD - back:CONV_PAD - back + sub, :] * conv_w_ref[t:t + 1, cols]
        return out

    for r0 in range(0, x_ref.shape[0], sub):
        rows = slice(r0, r0 + sub)
        h_buf[rows, :] = _rms_norm(x_ref[rows, :], g_ref[...]).astype(jnp.bfloat16)
        for jb in range(D_FF // FF_BLOCK):
            gate = conv_up(rows, slice(jb * FF_BLOCK, (jb + 1) * FF_BLOCK))
            val = conv_up(rows, slice(D_FF + jb * FF_BLOCK, D_FF + (jb + 1) * FF_BLOCK))
            act = 0.5 * gate * (1.0 + lax.erf(gate * (2.0 ** -0.5)))
            hid_buf[rows, jb * FF_BLOCK:(jb + 1) * FF_BLOCK] = (act * val).astype(jnp.bfloat16)
        for q0 in range(r0, r0 + sub, sub // 4):
            quarter = slice(q0, q0 + sub // 4)
            y = x_ref[quarter, :] + _dot(hid_buf[quarter, :], w_down_ref[...])
            o_ref[quarter, :] = _rms_norm(y, g_final_ref[...])


def _resident(shape):
    zeros = (0,) * len(shape)
    return pl.BlockSpec(shape, lambda i: zeros, pipeline_mode=pl.Buffered(1))


def _whole(shape):
    zeros = (0,) * len(shape)
    return pl.BlockSpec(shape, lambda i: zeros)


def _rel_vector(rel_bias):
    assert CHUNK - 1 <= MAX_REL <= WIN_KEYS - CHUNK - 1
    n_clipped = WIN_KEYS - CHUNK - MAX_REL
    far = jnp.broadcast_to(rel_bias[:, 2 * MAX_REL:], (ATTN_HEADS, n_clipped))
    near = rel_bias[:, MAX_REL - (CHUNK - 1):2 * MAX_REL][:, ::-1]
    unused = jnp.zeros((ATTN_HEADS, 1), rel_bias.dtype)
    return jnp.concatenate([unused, far, near], axis=1).astype(jnp.float32)


def _block_diag(w):
    g, c, _ = w.shape
    eye = jnp.eye(g, dtype=w.dtype)
    return (eye[:, None, :, None] * w[:, :, None, :]).reshape(g * c, g * c)


def kernel(x, mem, norm_mix_g, norm_mem_g, w_in, b_gate, w_pool, pool_scale, rel_bias, w_mem_kv,
           w_up_pool, w_up_attn, w_up_mem, w_out, norm_ffn_g, w_ffn_up, conv_w, conv_b, w_ffn_down,
           norm_final_g):
    batch, seq, d = x.shape
    assert batch == 1 and d == D_MODEL and seq % SEQ_TILE == 0
    assert norm_mix_g.shape[0] == 1, "single layer"
    bf16 = jnp.bfloat16
    f32 = jnp.float32
    ts = SEQ_TILE
    n_tiles = seq // ts
    mem_len = mem.shape[1]
    row = lambda a: a.reshape(1, -1).astype(f32)
    params = pltpu.CompilerParams(dimension_semantics=("arbitrary",), vmem_limit_bytes=VMEM_LIMIT)

    small_w = (w_up_pool[0], w_up_attn[0], w_up_mem[0], w_out[0], _block_diag(w_pool[0]))
    prep_in = (mem[0], row(norm_mem_g[0]), w_mem_kv[0])
    slab = pl.BlockSpec((D_MODEL // PREP_STEPS, w_in.shape[2]), lambda i: (i, 0))
    prep_out = pl.pallas_call(
        _prep_kernel,
        grid=(PREP_STEPS,),
        in_specs=[_resident(a.shape) for a in prep_in] + [slab] + [_resident(a.shape) for a in small_w],
        out_specs=[_whole((mem_len, MEM_WIDTH)), _whole((MEM_HEADS // 2, mem_len, 2 * PAIR)), slab]
                  + [_whole(a.shape) for a in small_w],
        out_shape=[jax.ShapeDtypeStruct((mem_len, MEM_WIDTH), bf16),
                   jax.ShapeDtypeStruct((MEM_HEADS // 2, mem_len, 2 * PAIR), bf16),
                   jax.ShapeDtypeStruct(w_in.shape[1:], bf16)]
                  + [jax.ShapeDtypeStruct(a.shape, bf16) for a in small_w],
        compiler_params=params,
        name="prep",
    )(*prep_in, w_in[0], *small_w)
    km, vm, w_in_bf16, w_up_pool_bf16, w_up_attn_bf16, w_up_mem_bf16, w_out_bf16, w_pool_bf16 = prep_out

    tile_spec = pl.BlockSpec((ts, D_MODEL), lambda i: (i, 0))
    mixer_inputs = (
        row(norm_mix_g[0]), w_in_bf16, row(b_gate[0]), w_pool_bf16,
        row(pool_scale[0]), _rel_vector(rel_bias[0]), km, vm,
        w_up_pool_bf16, w_up_attn_bf16, w_up_mem_bf16, w_out_bf16,
    )
    ffn_w = (w_ffn_up[0], w_ffn_down[0])
    slab_steps = (1, 2)
    slab_specs = [pl.BlockSpec((w.shape[0] * k // n_tiles, w.shape[1]), functools.partial(lambda i, k: (i // k, 0), k=k))
                  for w, k in zip(ffn_w, slab_steps)]
    assert all((w.shape[0] * k) % (n_tiles * BF16_ROWS) == 0 for w, k in zip(ffn_w, slab_steps))
    x1, w_up_bf16, w_down_bf16 = pl.pallas_call(
        _mixer_kernel,
        grid=(n_tiles,),
        in_specs=[pl.BlockSpec(memory_space=pltpu.SMEM), tile_spec] + [_resident(a.shape) for a in mixer_inputs] + slab_specs,
        out_specs=[tile_spec] + slab_specs,
        out_shape=[jax.ShapeDtypeStruct((seq, D_MODEL), f32)] + [jax.ShapeDtypeStruct(w.shape, bf16) for w in ffn_w],
        scratch_shapes=[
            pltpu.VMEM((ts, D_MODEL), bf16),
            pltpu.VMEM((POOL_TAIL + ts, POOL_WIDTH), f32),
            pltpu.VMEM((2, ts, ATTN_WIDTH), bf16),
            pltpu.VMEM((KV_CARRY + ts, ATTN_WIDTH), bf16),
            pltpu.VMEM((KV_CARRY + ts, 2 * ATTN_WIDTH), bf16),
            pltpu.VMEM((2, ts, MEM_WIDTH), bf16),
            pltpu.VMEM((ts, POOL_WIDTH), bf16),
            pltpu.VMEM((ts, ATTN_WIDTH), bf16),
            pltpu.VMEM((ts, MEM_WIDTH), bf16),
            pltpu.VMEM((ts, D_MODEL), bf16),
            pltpu.VMEM((ts, D_MODEL), f32),
            pltpu.VMEM((ts // MEM_ROWS, MEM_HEADS // 2, 2 * MEM_ROWS, mem_len), bf16),
            pltpu.VMEM((ATTN_HEADS // 2, 4 * CHUNK, WIN_KEYS), f32),
            pltpu.VMEM((ATTN_HEADS // 2, 4 * CHUNK, WIN_KEYS), f32),
            pltpu.VMEM((ATTN_HEADS // 2, 4 * CHUNK, WIN_KEYS), f32),
            pltpu.VMEM((ATTN_HEADS // 2, 4 * CHUNK, LANES), f32),
            pltpu.VMEM((ATTN_HEADS // 2, 4 * CHUNK, LANES), f32),
            pltpu.VMEM((ATTN_HEADS // 2, 4 * CHUNK, WIN_KEYS), bf16),
            pltpu.VMEM((ATTN_HEADS // 2, 4 * CHUNK, WIN_KEYS), bf16),
        ],
        compiler_params=params,
        name="mixer",
    )(jnp.zeros((1,), jnp.int32), x[0], *mixer_inputs, *ffn_w)

    ffn_inputs = (
        row(norm_ffn_g[0]), w_up_bf16, conv_w[0].astype(f32), row(conv_b[0]),
        w_down_bf16, row(norm_final_g),
    )
    tf = FFN_TILE
    ffn_tile_spec = pl.BlockSpec((tf, D_MODEL), lambda i: (i, 0))
    out = pl.pallas_call(
        _ffn_kernel,
        grid=(seq // tf,),
        in_specs=[ffn_tile_spec] + [_resident(a.shape) for a in ffn_inputs],
        out_specs=ffn_tile_spec,
        out_shape=jax.ShapeDtypeStruct((seq, D_MODEL), x.dtype),
        scratch_shapes=[
            pltpu.VMEM((tf, D_MODEL), bf16),
            pltpu.VMEM((CONV_PAD + FFN_SUB, FF_BLOCK), f32),
            pltpu.VMEM((CONV_PAD, 2 * D_FF), f32),
            pltpu.VMEM((tf, D_FF), bf16),
        ],
        compiler_params=params,
        name="ffn",
    )(x1, *ffn_inputs)
    return out[None]
```

```python
import functools

import jax
import jax.numpy as jnp
from jax import lax
from jax.experimental import pallas as pl
from jax.experimental.pallas import tpu as pltpu

D_MODEL = 1024
CHUNK = 64
HEAD_DIM = 64
POOL_WINDOWS = (2, 4, 8, 16)
POOL_GROUP_DIM = 64
POOL_WIDTH = len(POOL_WINDOWS) * POOL_GROUP_DIM
ATTN_HEADS = 8
ATTN_WIDTH = ATTN_HEADS * HEAD_DIM
BAND_CHUNKS = 9
BAND_KEYS = BAND_CHUNKS * CHUNK
MAX_REL = 128
MEM_HEADS = 4
MEM_WIDTH = MEM_HEADS * HEAD_DIM
N_BRANCH = 3
D_FF = 2816
CONV_WIDTH = 3
RMS_EPS = 1e-6
NEG_INF = -1e30

O_POOL = 0
O_Q = O_POOL + POOL_WIDTH
O_K = O_Q + ATTN_WIDTH
O_V = O_K + ATTN_WIDTH
O_QM = O_V + ATTN_WIDTH
O_GATE = O_QM + MEM_WIDTH

LANES = 128
SUBLANES = 8
BF16_ROWS = 2 * SUBLANES
PAIR = 2 * HEAD_DIM
POOL_TAIL = 16
WIN_KEYS = BAND_KEYS + CHUNK
KV_CARRY = BAND_KEYS - CHUNK
SEQ_TILE = 512
FFN_TILE = 1024
FFN_SUB = 1024
MEM_ROWS = 128
PREP_STEPS = 8
FF_BLOCK = 256
CONV_PAD = SUBLANES
VMEM_LIMIT = 60 * 1024 * 1024


def _rms_norm(x, g):
    y = x * lax.rsqrt(jnp.mean(x * x, axis=-1, keepdims=True) + RMS_EPS)
    return y * g


def _dot(a, b):
    return jnp.dot(a, b, preferred_element_type=jnp.float32)


def _dot_nt(a, b):
    return lax.dot_general(a, b, (((1,), (1,)), ((), ())), preferred_element_type=jnp.float32)


def _prep_kernel(mem_ref, g_ref, w_kv_ref, w_in_ref, *refs):
    n_small = (len(refs) - 3) // 2
    small_in = refs[:n_small]
    k_ref, v_ref, w_in_bf16_ref = refs[n_small:n_small + 3]
    small_out = refs[n_small + 3:]
    w_in_bf16_ref[...] = w_in_ref[...].astype(jnp.bfloat16)

    @pl.when(pl.program_id(0) == 0)
    def _first():
        for src, dst in zip(small_in, small_out):
            dst[...] = src[...].astype(jnp.bfloat16)
        mem_len = mem_ref.shape[0]
        mem_n = _rms_norm(mem_ref[...], g_ref[...]).astype(jnp.bfloat16)
        kv = _dot(mem_n, w_kv_ref[...].astype(jnp.bfloat16))
        k_ref[...] = kv[:, :MEM_WIDTH].astype(jnp.bfloat16)
        for p in range(MEM_HEADS // 2):
            v_ref[p, :, 0:PAIR] = kv[:, MEM_WIDTH + p * PAIR:MEM_WIDTH + (p + 1) * PAIR].astype(jnp.bfloat16)
            v_ref[p, :, PAIR:2 * PAIR] = jnp.ones((mem_len, PAIR), jnp.bfloat16)


def _mixer_kernel(zero_ref, x_ref, g_ref, w_in_ref, b_gate_ref, w_pool_ref, pool_scale_ref, rel_vec_ref,
                  km_ref, vm_ref, w_up_pool_ref, w_up_attn_ref, w_up_mem_ref, w_out_ref,
                  w_ffn_up_ref, w_ffn_down_ref,
                  o_ref, w_ffn_up_bf16_ref, w_ffn_down_bf16_ref,
                  h_buf, u_buf, q_buf, k_buf, v_buf, qm_buf, pool_buf, attn_buf, mem_buf, merged_buf,
                  part_buf, pm_buf, bias_buf, s_buf0, s_buf1, m_buf0, m_buf1, p_buf0, p_buf1):
    s_buf, m_buf, p_buf = (s_buf0, s_buf1), (m_buf0, m_buf1), (p_buf0, p_buf1)
    ts = x_ref.shape[0]
    tile = pl.program_id(0)

    @pl.when(tile == 0)
    def _init():
        u_buf[0:POOL_TAIL, :] = jnp.zeros((POOL_TAIL, POOL_WIDTH), jnp.float32)
        k_buf[0:KV_CARRY, :] = jnp.zeros((KV_CARRY, ATTN_WIDTH), jnp.bfloat16)
        ext_lane = lax.broadcasted_iota(jnp.int32, v_buf.shape, 1) % (2 * PAIR)
        v_buf[...] = jnp.where(ext_lane >= PAIR, 1.0, 0.0).astype(jnp.bfloat16)
        win_lane = lax.broadcasted_iota(jnp.int32, (ATTN_HEADS, WIN_KEYS), 1)
        vecs = rel_vec_ref[...]
        for qi in range(CHUNK):
            first = jnp.where(win_lane >= BAND_KEYS, NEG_INF, pltpu.roll(vecs, qi + BAND_KEYS, axis=1))
            second = jnp.where(win_lane < CHUNK, NEG_INF, pltpu.roll(vecs, qi, axis=1) if qi else vecs)
            for h in range(ATTN_HEADS):
                r = (h % 2) * 2 * CHUNK + qi
                bias_buf[h // 2, r:r + 1, :] = first[h:h + 1, :]
                bias_buf[h // 2, r + CHUNK:r + CHUNK + 1, :] = second[h:h + 1, :]

    w_ffn_up_bf16_ref[...] = w_ffn_up_ref[...].astype(jnp.bfloat16)
    w_ffn_down_bf16_ref[...] = w_ffn_down_ref[...].astype(jnp.bfloat16)

    h_buf[...] = _rms_norm(x_ref[...], g_ref[...]).astype(jnp.bfloat16)

    u_buf[POOL_TAIL:POOL_TAIL + ts, :] = _dot(h_buf[...], w_in_ref[:, O_POOL:O_Q])
    def split_heads(q, even_ref, odd_ref):
        odd = (lax.broadcasted_iota(jnp.int32, q.shape, 1) // HEAD_DIM) % 2 == 1
        q = q * (HEAD_DIM ** -0.5)
        even_ref[...] = jnp.where(odd, 0.0, q).astype(jnp.bfloat16)
        odd_ref[...] = jnp.where(odd, q, 0.0).astype(jnp.bfloat16)

    split_heads(_dot(h_buf[...], w_in_ref[:, O_Q:O_K]), q_buf.at[0], q_buf.at[1])
    k_buf[KV_CARRY:KV_CARRY + ts, :] = _dot(h_buf[...], w_in_ref[:, O_K:O_V]).astype(jnp.bfloat16)
    v = _dot(h_buf[...], w_in_ref[:, O_V:O_QM]).astype(jnp.bfloat16)
    for p in range(ATTN_HEADS // 2):
        v_buf[KV_CARRY:KV_CARRY + ts, p * 2 * PAIR:p * 2 * PAIR + PAIR] = v[:, p * PAIR:(p + 1) * PAIR]
    split_heads(_dot(h_buf[...], w_in_ref[:, O_QM:O_GATE]), qm_buf.at[0], qm_buf.at[1])

    u = u_buf[POOL_TAIL:POOL_TAIL + ts, :]
    lane_group = lax.broadcasted_iota(jnp.int32, (ts, POOL_WIDTH), 1) // POOL_GROUP_DIM
    run = u
    win_sum = jnp.zeros_like(u)
    for j in range(1, POOL_TAIL + 1):
        if j in POOL_WINDOWS:
            win_sum = jnp.where(lane_group == POOL_WINDOWS.index(j), run, win_sum)
        if j < POOL_TAIL:
            run = run + u_buf[POOL_TAIL - j:POOL_TAIL - j + ts, :]
    window = jnp.left_shift(2, lane_group)
    pos1 = tile * ts + lax.broadcasted_iota(jnp.int32, (ts, POOL_WIDTH), 0) + 1
    cnt = jnp.minimum(pos1, window).astype(jnp.float32)
    pooled = (win_sum / cnt - u).astype(jnp.bfloat16)
    pool_buf[...] = (_dot(pooled, w_pool_ref[...]) * pool_scale_ref[...]).astype(jnp.bfloat16)
    u_buf[0:POOL_TAIL, :] = u_buf[ts:ts + POOL_TAIL, :]

    def pair_cols(h):
        return slice((h // 2) * PAIR, (h // 2 + 1) * PAIR)

    staged = zero_ref[0]

    def pair_output(acc):
        half = acc.shape[0] // 2
        out = acc[:, :PAIR] / acc[:, PAIR:]
        lane_is_odd = lax.broadcasted_iota(jnp.int32, (half, PAIR), 1) >= HEAD_DIM
        return jnp.where(lane_is_odd, out[half:], out[:half]).astype(jnp.bfloat16)

    mem_len = km_ref.shape[0]
    n_mem = ts // MEM_ROWS

    def mem_score(r):
        r0 = r * MEM_ROWS
        for h in range(MEM_HEADS):
            s = _dot_nt(qm_buf[h % 2, r0:r0 + MEM_ROWS, pair_cols(h)], km_ref[:, pair_cols(h)])
            e = jnp.exp(s - jnp.max(s, axis=-1, keepdims=True))
            pm_buf[r, h // 2, (h % 2) * MEM_ROWS:(h % 2 + 1) * MEM_ROWS, :] = e.astype(jnp.bfloat16)

    def mem_value(r):
        r0 = r * MEM_ROWS
        for p in range(MEM_HEADS // 2):
            acc = _dot(pm_buf[staged + r, p], vm_ref[p])
            mem_buf[r0:r0 + MEM_ROWS, pair_cols(2 * p)] = pair_output(acc)

    merge_blk = 2 * LANES
    n_merge = D_MODEL // merge_blk

    def merge_cols(nb):
        return slice(nb * merge_blk, (nb + 1) * merge_blk)

    def gated(b, branch_buf, w_up_ref, nb):
        lo = b * D_MODEL + nb * merge_blk
        logits = _dot(h_buf[...], w_in_ref[:, O_GATE + lo:O_GATE + lo + merge_blk]) + b_gate_ref[:, lo:lo + merge_blk]
        return jax.nn.sigmoid(logits) * _dot(branch_buf[...], w_up_ref[:, merge_cols(nb)])

    win_col = lax.broadcasted_iota(jnp.int32, (1, WIN_KEYS), 1)
    pad_rows = jnp.where(tile == 0, KV_CARRY, 0)

    def ext_cols(p):
        return slice(p * 2 * PAIR, (p + 1) * 2 * PAIR)

    n_k = WIN_KEYS // LANES
    n_cp = ts // (2 * CHUNK)

    def score_phase(cp):
        base = cp * 2 * CHUNK
        slot = cp % 2
        pen = jnp.where(win_col + base < pad_rows, NEG_INF, 0.0)
        for p in range(ATTN_HEADS // 2):
            q_rows = jnp.concatenate([q_buf[0, base:base + 2 * CHUNK, pair_cols(2 * p)],
                                      q_buf[1, base:base + 2 * CHUNK, pair_cols(2 * p)]], axis=0)
            s = _dot_nt(q_rows, k_buf[base:base + WIN_KEYS, pair_cols(2 * p)]) + bias_buf[p] + pen
            s_buf[slot][p] = s
            part = s[:, 0:LANES]
            for k in range(1, n_k):
                part = jnp.maximum(part, s[:, k * LANES:(k + 1) * LANES])
            m_buf[slot][p] = jnp.broadcast_to(jnp.max(part, axis=-1, keepdims=True), (4 * CHUNK, LANES))

    def softmax_phase(cp):
        slot = cp % 2
        for p in range(ATTN_HEADS // 2):
            m = m_buf[slot][staged + p]
            for k in range(n_k):
                cols = slice(k * LANES, (k + 1) * LANES)
                p_buf[slot][p, :, cols] = jnp.exp(s_buf[slot][staged + p, :, cols] - m).astype(jnp.bfloat16)

    def value_phase(cp):
        base = cp * 2 * CHUNK
        slot = cp % 2
        for p in range(ATTN_HEADS // 2):
            acc = _dot(p_buf[slot][staged + p], v_buf[base:base + WIN_KEYS, ext_cols(p)])
            attn_buf[base:base + 2 * CHUNK, pair_cols(2 * p)] = pair_output(acc)

    mem_steps = 2
    mem_per_step = n_mem // mem_steps
    assert mem_steps + n_merge <= n_cp + 2
    for step in range(n_cp + 2):
        mem_blocks = range(step * mem_per_step, (step + 1) * mem_per_step) if step < mem_steps else ()
        for r in mem_blocks:
            mem_score(r)
        if step < n_cp:
            score_phase(step)
        if 0 <= step - mem_steps < n_merge:
            nb = step - mem_steps
            part_buf[:, merge_cols(nb)] = (gated(0, pool_buf, w_up_pool_ref, nb)
                                           + gated(2, mem_buf, w_up_mem_ref, nb))
        if 0 <= step - 1 < n_cp:
            softmax_phase(step - 1)
        if 0 <= step - 2 < n_cp:
            value_phase(step - 2)
        for r in mem_blocks:
            mem_value(r)

    k_buf[0:KV_CARRY, :] = k_buf[ts:ts + KV_CARRY, :]
    for p in range(ATTN_HEADS // 2):
        vcols = slice(p * 2 * PAIR, p * 2 * PAIR + PAIR)
        v_buf[0:KV_CARRY, vcols] = v_buf[ts:ts + KV_CARRY, vcols]

    for nb in range(n_merge):
        merged = part_buf[:, merge_cols(nb)] + gated(1, attn_buf, w_up_attn_ref, nb)
        merged_buf[:, merge_cols(nb)] = merged.astype(jnp.bfloat16)
    o_ref[...] = x_ref[...] + _dot(merged_buf[...], w_out_ref[...])


def _ffn_kernel(x_ref, g_ref, w_up_ref, conv_w_ref, conv_b_ref, w_down_ref, g_final_ref, o_ref,
                h_buf, a_buf, tail_buf, hid_buf):
    sub = a_buf.shape[0] - CONV_PAD

    @pl.when(pl.program_id(0) == 0)
    def _init():
        tail_buf[...] = jnp.zeros_like(tail_buf)

    def conv_up(rows, cols):
        a = _dot(h_buf[rows, :], w_up_ref[:, cols])
        a_buf[0:CONV_PAD, :] = tail_buf[:, cols]
        a_buf[CONV_PAD:CONV_PAD + sub, :] = a
        tail_buf[:, cols] = a_buf[sub:sub + CONV_PAD, :]
        out = conv_b_ref[:, cols] + a * conv_w_ref[CONV_WIDTH - 1:CONV_WIDTH, cols]
        for t in range(CONV_WIDTH - 1):
            back = CONV_WIDTH - 1 - t
            out = out + a_buf[CONV_PAD - back:CONV_PAD - back + sub, :] * conv_w_ref[t:t + 1, cols]
        return out

    for r0 in range(0, x_ref.shape[0], sub):
        rows = slice(r0, r0 + sub)
        h_buf[rows, :] = _rms_norm(x_ref[rows, :], g_ref[...]).astype(jnp.bfloat16)
        for jb in range(D_FF // FF_BLOCK):
            gate = conv_up(rows, slice(jb * FF_BLOCK, (jb + 1) * FF_BLOCK))
            val = conv_up(rows, slice(D_FF + jb * FF_BLOCK, D_FF + (jb + 1) * FF_BLOCK))
            act = 0.5 * gate * (1.0 + lax.erf(gate * (2.0 ** -0.5)))
            hid_buf[rows, jb * FF_BLOCK:(jb + 1) * FF_BLOCK] = (act * val).astype(jnp.bfloat16)
        for q0 in range(r0, r0 + sub, sub // 4):
            quarter = slice(q0, q0 + sub // 4)
            y = x_ref[quarter, :] + _dot(hid_buf[quarter, :], w_down_ref[...])
            o_ref[quarter, :] = _rms_norm(y, g_final_ref[...])


def _resident(shape):
    zeros = (0,) * len(shape)
    return pl.BlockSpec(shape, lambda i: zeros, pipeline_mode=pl.Buffered(1))


def _whole(shape):
    zeros = (0,) * len(shape)
    return pl.BlockSpec(shape, lambda i: zeros)


def _rel_vector(rel_bias):
    assert CHUNK - 1 <= MAX_REL <= WIN_KEYS - CHUNK - 1
    n_clipped = WIN_KEYS - CHUNK - MAX_REL
    far = jnp.broadcast_to(rel_bias[:, 2 * MAX_REL:], (ATTN_HEADS, n_clipped))
    near = rel_bias[:, MAX_REL - (CHUNK - 1):2 * MAX_REL][:, ::-1]
    unused = jnp.zeros((ATTN_HEADS, 1), rel_bias.dtype)
    return jnp.concatenate([unused, far, near], axis=1).astype(jnp.float32)


def _block_diag(w):
    g, c, _ = w.shape
    eye = jnp.eye(g, dtype=w.dtype)
    return (eye[:, None, :, None] * w[:, :, None, :]).reshape(g * c, g * c)


def kernel(x, mem, norm_mix_g, norm_mem_g, w_in, b_gate, w_pool, pool_scale, rel_bias, w_mem_kv,
           w_up_pool, w_up_attn, w_up_mem, w_out, norm_ffn_g, w_ffn_up, conv_w, conv_b, w_ffn_down,
           norm_final_g):
    batch, seq, d = x.shape
    assert batch == 1 and d == D_MODEL and seq % SEQ_TILE == 0
    assert norm_mix_g.shape[0] == 1, "single layer"
    bf16 = jnp.bfloat16
    f32 = jnp.float32
    ts = SEQ_TILE
    n_tiles = seq // ts
    mem_len = mem.shape[1]
    row = lambda a: a.reshape(1, -1).astype(f32)
    params = pltpu.CompilerParams(dimension_semantics=("arbitrary",), vmem_limit_bytes=VMEM_LIMIT)

    small_w = (w_up_pool[0], w_up_attn[0], w_up_mem[0], w_out[0], _block_diag(w_pool[0]))
    prep_in = (mem[0], row(norm_mem_g[0]), w_mem_kv[0])
    slab = pl.BlockSpec((D_MODEL // PREP_STEPS, w_in.shape[2]), lambda i: (i, 0))
    prep_out = pl.pallas_call(
        _prep_kernel,
        grid=(PREP_STEPS,),
        in_specs=[_resident(a.shape) for a in prep_in] + [slab] + [_resident(a.shape) for a in small_w],
        out_specs=[_whole((mem_len, MEM_WIDTH)), _whole((MEM_HEADS // 2, mem_len, 2 * PAIR)), slab]
                  + [_whole(a.shape) for a in small_w],
        out_shape=[jax.ShapeDtypeStruct((mem_len, MEM_WIDTH), bf16),
                   jax.ShapeDtypeStruct((MEM_HEADS // 2, mem_len, 2 * PAIR), bf16),
                   jax.ShapeDtypeStruct(w_in.shape[1:], bf16)]
                  + [jax.ShapeDtypeStruct(a.shape, bf16) for a in small_w],
        compiler_params=params,
        name="prep",
    )(*prep_in, w_in[0], *small_w)
    km, vm, w_in_bf16, w_up_pool_bf16, w_up_attn_bf16, w_up_mem_bf16, w_out_bf16, w_pool_bf16 = prep_out

    tile_spec = pl.BlockSpec((ts, D_MODEL), lambda i: (i, 0))
    mixer_inputs = (
        row(norm_mix_g[0]), w_in_bf16, row(b_gate[0]), w_pool_bf16,
        row(pool_scale[0]), _rel_vector(rel_bias[0]), km, vm,
        w_up_pool_bf16, w_up_attn_bf16, w_up_mem_bf16, w_out_bf16,
    )
    ffn_w = (w_ffn_up[0], w_ffn_down[0])
    slab_steps = (1, 2)
    slab_specs = [pl.BlockSpec((w.shape[0] * k // n_tiles, w.shape[1]), functools.partial(lambda i, k: (i // k, 0), k=k))
                  for w, k in zip(ffn_w, slab_steps)]
    assert all((w.shape[0] * k) % (n_tiles * BF16_ROWS) == 0 for w, k in zip(ffn_w, slab_steps))
    x1, w_up_bf16, w_down_bf16 = pl.pallas_call(
        _mixer_kernel,
        grid=(n_tiles,),
        in_specs=[pl.BlockSpec(memory_space=pltpu.SMEM), tile_spec] + [_resident(a.shape) for a in mixer_inputs] + slab_specs,
        out_specs=[tile_spec] + slab_specs,
        out_shape=[jax.ShapeDtypeStruct((seq, D_MODEL), f32)] + [jax.ShapeDtypeStruct(w.shape, bf16) for w in ffn_w],
        scratch_shapes=[
            pltpu.VMEM((ts, D_MODEL), bf16),
            pltpu.VMEM((POOL_TAIL + ts, POOL_WIDTH), f32),
            pltpu.VMEM((2, ts, ATTN_WIDTH), bf16),
            pltpu.VMEM((KV_CARRY + ts, ATTN_WIDTH), bf16),
            pltpu.VMEM((KV_CARRY + ts, 2 * ATTN_WIDTH), bf16),
            pltpu.VMEM((2, ts, MEM_WIDTH), bf16),
            pltpu.VMEM((ts, POOL_WIDTH), bf16),
            pltpu.VMEM((ts, ATTN_WIDTH), bf16),
            pltpu.VMEM((ts, MEM_WIDTH), bf16),
            pltpu.VMEM((ts, D_MODEL), bf16),
            pltpu.VMEM((ts, D_MODEL), f32),
            pltpu.VMEM((ts // MEM_ROWS, MEM_HEADS // 2, 2 * MEM_ROWS, mem_len), bf16),
            pltpu.VMEM((ATTN_HEADS // 2, 4 * CHUNK, WIN_KEYS), f32),
            pltpu.VMEM((ATTN_HEADS // 2, 4 * CHUNK, WIN_KEYS), f32),
            pltpu.VMEM((ATTN_HEADS // 2, 4 * CHUNK, WIN_KEYS), f32),
            pltpu.VMEM((ATTN_HEADS // 2, 4 * CHUNK, LANES), f32),
            pltpu.VMEM((ATTN_HEADS // 2, 4 * CHUNK, LANES), f32),
            pltpu.VMEM((ATTN_HEADS // 2, 4 * CHUNK, WIN_KEYS), bf16),
            pltpu.VMEM((ATTN_HEADS // 2, 4 * CHUNK, WIN_KEYS), bf16),
        ],
        compiler_params=params,
        name="mixer",
    )(jnp.zeros((1,), jnp.int32), x[0], *mixer_inputs, *ffn_w)

    ffn_inputs = (
        row(norm_ffn_g[0]), w_up_bf16, conv_w[0].astype(f32), row(conv_b[0]),
        w_down_bf16, row(norm_final_g),
    )
    tf = FFN_TILE
    ffn_tile_spec = pl.BlockSpec((tf, D_MODEL), lambda i: (i, 0))
    out = pl.pallas_call(
        _ffn_kernel,
        grid=(seq // tf,),
        in_specs=[ffn_tile_spec] + [_resident(a.shape) for a in ffn_inputs],
        out_specs=ffn_tile_spec,
        out_shape=jax.ShapeDtypeStruct((seq, D_MODEL), x.dtype),
        scratch_shapes=[
            pltpu.VMEM((tf, D_MODEL), bf16),
            pltpu.VMEM((CONV_PAD + FFN_SUB, FF_BLOCK), f32),
            pltpu.VMEM((CONV_PAD, 2 * D_FF), f32),
            pltpu.VMEM((tf, D_FF), bf16),
        ],
        compiler_params=params,
        name="ffn",
    )(x1, *ffn_inputs)
    return out[None]
```

```python
import functools

import jax
import jax.numpy as jnp
from jax import lax
from jax.experimental import pallas as pl
from jax.experimental.pallas import tpu as pltpu

D_MODEL = 1024
CHUNK = 64
HEAD_DIM = 64
POOL_WINDOWS = (2, 4, 8, 16)
POOL_GROUP_DIM = 64
POOL_WIDTH = len(POOL_WINDOWS) * POOL_GROUP_DIM
ATTN_HEADS = 8
ATTN_WIDTH = ATTN_HEADS * HEAD_DIM
BAND_CHUNKS = 9
BAND_KEYS = BAND_CHUNKS * CHUNK
MAX_REL = 128
MEM_HEADS = 4
MEM_WIDTH = MEM_HEADS * HEAD_DIM
N_BRANCH = 3
D_FF = 2816
CONV_WIDTH = 3
RMS_EPS = 1e-6
NEG_INF = -1e30

O_POOL = 0
O_Q = O_POOL + POOL_WIDTH
O_K = O_Q + ATTN_WIDTH
O_V = O_K + ATTN_WIDTH
O_QM = O_V + ATTN_WIDTH
O_GATE = O_QM + MEM_WIDTH

LANES = 128
SUBLANES = 8
BF16_ROWS = 2 * SUBLANES
PAIR = 2 * HEAD_DIM
POOL_TAIL = 16
WIN_KEYS = BAND_KEYS + CHUNK
KV_CARRY = BAND_KEYS - CHUNK
SEQ_TILE = 512
FFN_TILE = 1024
FFN_SUB = 1024
MEM_ROWS = 128
PREP_STEPS = 8
FF_BLOCK = 256
CONV_PAD = SUBLANES
VMEM_LIMIT = 60 * 1024 * 1024


def _rms_norm(x, g):
    y = x * lax.rsqrt(jnp.mean(x * x, axis=-1, keepdims=True) + RMS_EPS)
    return y * g


def _dot(a, b):
    return jnp.dot(a, b, preferred_element_type=jnp.float32)


def _dot_nt(a, b):
    return lax.dot_general(a, b, (((1,), (1,)), ((), ())), preferred_element_type=jnp.float32)


def _prep_kernel(mem_ref, g_ref, w_kv_ref, w_in_ref, *refs):
    n_small = (len(refs) - 3) // 2
    small_in = refs[:n_small]
    k_ref, v_ref, w_in_bf16_ref = refs[n_small:n_small + 3]
    small_out = refs[n_small + 3:]
    w_in_bf16_ref[...] = w_in_ref[...].astype(jnp.bfloat16)

    @pl.when(pl.program_id(0) == 0)
    def _first():
        for src, dst in zip(small_in, small_out):
            dst[...] = src[...].astype(jnp.bfloat16)
        mem_len = mem_ref.shape[0]
        mem_n = _rms_norm(mem_ref[...], g_ref[...]).astype(jnp.bfloat16)
        kv = _dot(mem_n, w_kv_ref[...].astype(jnp.bfloat16))
        k_ref[...] = kv[:, :MEM_WIDTH].astype(jnp.bfloat16)
        for p in range(MEM_HEADS // 2):
            v_ref[p, :, 0:PAIR] = kv[:, MEM_WIDTH + p * PAIR:MEM_WIDTH + (p + 1) * PAIR].astype(jnp.bfloat16)
            v_ref[p, :, PAIR:2 * PAIR] = jnp.ones((mem_len, PAIR), jnp.bfloat16)


def _mixer_kernel(zero_ref, x_ref, g_ref, w_in_ref, b_gate_ref, w_pool_ref, pool_scale_ref, rel_vec_ref,
                  km_ref, vm_ref, w_up_pool_ref, w_up_attn_ref, w_up_mem_ref, w_out_ref,
                  w_ffn_up_ref, w_ffn_down_ref,
                  o_ref, w_ffn_up_bf16_ref, w_ffn_down_bf16_ref,
                  h_buf, u_buf, q_buf, k_buf, v_buf, qm_buf, pool_buf, attn_buf, mem_buf, merged_buf,
                  part_buf, pm_buf, bias_buf, s_buf0, s_buf1, m_buf0, m_buf1, p_buf0, p_buf1):
    s_buf, m_buf, p_buf = (s_buf0, s_buf1), (m_buf0, m_buf1), (p_buf0, p_buf1)
    ts = x_ref.shape[0]
    tile = pl.program_id(0)

    @pl.when(tile == 0)
    def _init():
        u_buf[0:POOL_TAIL, :] = jnp.zeros((POOL_TAIL, POOL_WIDTH), jnp.float32)
        k_buf[0:KV_CARRY, :] = jnp.zeros((KV_CARRY, ATTN_WIDTH), jnp.bfloat16)
        ext_lane = lax.broadcasted_iota(jnp.int32, v_buf.shape, 1) % (2 * PAIR)
        v_buf[...] = jnp.where(ext_lane >= PAIR, 1.0, 0.0).astype(jnp.bfloat16)
        win_lane = lax.broadcasted_iota(jnp.int32, (ATTN_HEADS, WIN_KEYS), 1)
        vecs = rel_vec_ref[...]
        for qi in range(CHUNK):
            first = jnp.where(win_lane >= BAND_KEYS, NEG_INF, pltpu.roll(vecs, qi + BAND_KEYS, axis=1))
            second = jnp.where(win_lane < CHUNK, NEG_INF, pltpu.roll(vecs, qi, axis=1) if qi else vecs)
            for h in range(ATTN_HEADS):
                r = (h % 2) * 2 * CHUNK + qi
                bias_buf[h // 2, r:r + 1, :] = first[h:h + 1, :]
                bias_buf[h // 2, r + CHUNK:r + CHUNK + 1, :] = second[h:h + 1, :]

    w_ffn_up_bf16_ref[...] = w_ffn_up_ref[...].astype(jnp.bfloat16)
    w_ffn_down_bf16_ref[...] = w_ffn_down_ref[...].astype(jnp.bfloat16)

    h_buf[...] = _rms_norm(x_ref[...], g_ref[...]).astype(jnp.bfloat16)

    u_buf[POOL_TAIL:POOL_TAIL + ts, :] = _dot(h_buf[...], w_in_ref[:, O_POOL:O_Q])
    def split_heads(q, even_ref, odd_ref):
        odd = (lax.broadcasted_iota(jnp.int32, q.shape, 1) // HEAD_DIM) % 2 == 1
        q = q * (HEAD_DIM ** -0.5)
        even_ref[...] = jnp.where(odd, 0.0, q).astype(jnp.bfloat16)
        odd_ref[...] = jnp.where(odd, q, 0.0).astype(jnp.bfloat16)

    split_heads(_dot(h_buf[...], w_in_ref[:, O_Q:O_K]), q_buf.at[0], q_buf.at[1])
    k_buf[KV_CARRY:KV_CARRY + ts, :] = _dot(h_buf[...], w_in_ref[:, O_K:O_V]).astype(jnp.bfloat16)
    v = _dot(h_buf[...], w_in_ref[:, O_V:O_QM]).astype(jnp.bfloat16)
    for p in range(ATTN_HEADS // 2):
        v_buf[KV_CARRY:KV_CARRY + ts, p * 2 * PAIR:p * 2 * PAIR + PAIR] = v[:, p * PAIR:(p + 1) * PAIR]
    split_heads(_dot(h_buf[...], w_in_ref[:, O_QM:O_GATE]), qm_buf.at[0], qm_buf.at[1])

    u = u_buf[POOL_TAIL:POOL_TAIL + ts, :]
    lane_group = lax.broadcasted_iota(jnp.int32, (ts, POOL_WIDTH), 1) // POOL_GROUP_DIM
    run = u
    win_sum = jnp.zeros_like(u)
    for j in range(1, POOL_TAIL + 1):
        if j in POOL_WINDOWS:
            win_sum = jnp.where(lane_group == POOL_WINDOWS.index(j), run, win_sum)
        if j < POOL_TAIL:
            run = run + u_buf[POOL_TAIL - j:POOL_TAIL - j + ts, :]
    window = jnp.left_shift(2, lane_group)
    pos1 = tile * ts + lax.broadcasted_iota(jnp.int32, (ts, POOL_WIDTH), 0) + 1
    cnt = jnp.minimum(pos1, window).astype(jnp.float32)
    pooled = (win_sum / cnt - u).astype(jnp.bfloat16)
    pool_buf[...] = (_dot(pooled, w_pool_ref[...]) * pool_scale_ref[...]).astype(jnp.bfloat16)
    u_buf[0:POOL_TAIL, :] = u_buf[ts:ts + POOL_TAIL, :]

    def pair_cols(h):
        return slice((h // 2) * PAIR, (h // 2 + 1) * PAIR)

    staged = zero_ref[0]

    def pair_output(acc):
        half = acc.shape[0] // 2
        out = acc[:, :PAIR] / acc[:, PAIR:]
        lane_is_odd = lax.broadcasted_iota(jnp.int32, (half, PAIR), 1) >= HEAD_DIM
        return jnp.where(lane_is_odd, out[half:], out[:half]).astype(jnp.bfloat16)

    mem_len = km_ref.shape[0]
    n_mem = ts // MEM_ROWS

    def mem_score(r):
        r0 = r * MEM_ROWS
        for h in range(MEM_HEADS):
            s = _dot_nt(qm_buf[h % 2, r0:r0 + MEM_ROWS, pair_cols(h)], km_ref[:, pair_cols(h)])
            e = jnp.exp(s - jnp.max(s, axis=-1, keepdims=True))
            pm_buf[r, h // 2, (h % 2) * MEM_ROWS:(h % 2 + 1) * MEM_ROWS, :] = e.astype(jnp.bfloat16)

    def mem_value(r):
        r0 = r * MEM_ROWS
        for p in range(MEM_HEADS // 2):
            acc = _dot(pm_buf[staged + r, p], vm_ref[p])
            mem_buf[r0:r0 + MEM_ROWS, pair_cols(2 * p)] = pair_output(acc)

    merge_blk = 2 * LANES
    n_merge = D_MODEL // merge_blk

    def merge_cols(nb):
        return slice(nb * merge_blk, (nb + 1) * merge_blk)

    def gated(b, branch_buf, w_up_ref, nb):
        lo = b * D_MODEL + nb * merge_blk
        logits = _dot(h_buf[...], w_in_ref[:, O_GATE + lo:O_GATE + lo + merge_blk]) + b_gate_ref[:, lo:lo + merge_blk]
        return jax.nn.sigmoid(logits) * _dot(branch_buf[...], w_up_ref[:, merge_cols(nb)])

    win_col = lax.broadcasted_iota(jnp.int32, (1, WIN_KEYS), 1)
    pad_rows = jnp.where(tile == 0, KV_CARRY, 0)

    def ext_cols(p):
        return slice(p * 2 * PAIR, (p + 1) * 2 * PAIR)

    n_k = WIN_KEYS // LANES
    n_cp = ts // (2 * CHUNK)

    def score_phase(cp):
        base = cp * 2 * CHUNK
        slot = cp % 2
        pen = jnp.where(win_col + base < pad_rows, NEG_INF, 0.0)
        for p in range(ATTN_HEADS // 2):
            q_rows = jnp.concatenate([q_buf[0, base:base + 2 * CHUNK, pair_cols(2 * p)],
                                      q_buf[1, base:base + 2 * CHUNK, pair_cols(2 * p)]], axis=0)
            s = _dot_nt(q_rows, k_buf[base:base + WIN_KEYS, pair_cols(2 * p)]) + bias_buf[p] + pen
            s_buf[slot][p] = s
            part = s[:, 0:LANES]
            for k in range(1, n_k):
                part = jnp.maximum(part, s[:, k * LANES:(k + 1) * LANES])
            m_buf[slot][p] = jnp.broadcast_to(jnp.max(part, axis=-1, keepdims=True), (4 * CHUNK, LANES))

    def softmax_phase(cp):
        slot = cp % 2
        for p in range(ATTN_HEADS // 2):
            m = m_buf[slot][staged + p]
            for k in range(n_k):
                cols = slice(k * LANES, (k + 1) * LANES)
                p_buf[slot][p, :, cols] = jnp.exp(s_buf[slot][staged + p, :, cols] - m).astype(jnp.bfloat16)

    def value_phase(cp):
        base = cp * 2 * CHUNK
        slot = cp % 2
        for p in range(ATTN_HEADS // 2):
            acc = _dot(p_buf[slot][staged + p], v_buf[base:base + WIN_KEYS, ext_cols(p)])
            attn_buf[base:base + 2 * CHUNK, pair_cols(2 * p)] = pair_output(acc)

    mem_steps = 2
    mem_per_step = n_mem // mem_steps
    assert mem_steps + n_merge <= n_cp + 2
    for step in range(n_cp + 2):
        mem_blocks = range(step * mem_per_step, (step + 1) * mem_per_step) if step < mem_steps else ()
        for r in mem_blocks:
            mem_score(r)
        if step < n_cp:
            score_phase(step)
        if 0 <= step - mem_steps < n_merge:
            nb = step - mem_steps
            part_buf[:, merge_cols(nb)] = (gated(0, pool_buf, w_up_pool_ref, nb)
                                           + gated(2, mem_buf, w_up_mem_ref, nb))
        if 0 <= step - 1 < n_cp:
            softmax_phase(step - 1)
        if 0 <= step - 2 < n_cp:
            value_phase(step - 2)
        for r in mem_blocks:
            mem_value(r)

    k_buf[0:KV_CARRY, :] = k_buf[ts:ts + KV_CARRY, :]
    for p in range(ATTN_HEADS // 2):
        vcols = slice(p * 2 * PAIR, p * 2 * PAIR + PAIR)
        v_buf[0:KV_CARRY, vcols] = v_buf[ts:ts + KV_CARRY, vcols]

    for nb in range(n_merge):
        merged = part_buf[:, merge_cols(nb)] + gated(1, attn_buf, w_up_attn_ref, nb)
        merged_buf[:, merge_cols(nb)] = merged.astype(jnp.bfloat16)
    o_ref[...] = x_ref[...] + _dot(merged_buf[...], w_out_ref[...])


def _ffn_kernel(x_ref, g_ref, w_up_ref, conv_w_ref, conv_b_ref, w_down_ref, g_final_ref, o_ref,
                h_buf, a_buf, tail_buf, hid_buf):
    sub = a_buf.shape[0] - CONV_PAD

    @pl.when(pl.program_id(0) == 0)
    def _init():
        tail_buf[...] = jnp.zeros_like(tail_buf)

    def conv_up(rows, cols):
        a = _dot(h_buf[rows, :], w_up_ref[:, cols])
        a_buf[0:CONV_PAD, :] = tail_buf[:, cols]
        a_buf[CONV_PAD:CONV_PAD + sub, :] = a
        tail_buf[:, cols] = a_buf[sub:sub + CONV_PAD, :]
        out = conv_b_ref[:, cols] + a * conv_w_ref[CONV_WIDTH - 1:CONV_WIDTH, cols]
        for t in range(CONV_WIDTH - 1):
            back = CONV_WIDTH - 1 - t
            out = out + a_buf[CONV_PAD - back:CONV_PAD - back + sub, :] * conv_w_ref[t:t + 1, cols]
        return out

    for r0 in range(0, x_ref.shape[0], sub):
        rows = slice(r0, r0 + sub)
        h_buf[rows, :] = _rms_norm(x_ref[rows, :], g_ref[...]).astype(jnp.bfloat16)
        for jb in range(D_FF // FF_BLOCK):
            gate = conv_up(rows, slice(jb * FF_BLOCK, (jb + 1) * FF_BLOCK))
            val = conv_up(rows, slice(D_FF + jb * FF_BLOCK, D_FF + (jb + 1) * FF_BLOCK))
            act = 0.5 * gate * (1.0 + lax.erf(gate * (2.0 ** -0.5)))
            hid_buf[rows, jb * FF_BLOCK:(jb + 1) * FF_BLOCK] = (act * val).astype(jnp.bfloat16)
        for q0 in range(r0, r0 + sub, sub // 4):
            quarter = slice(q0, q0 + sub // 4)
            y = x_ref[quarter, :] + _dot(hid_buf[quarter, :], w_down_ref[...])
            o_ref[quarter, :] = _rms_norm(y, g_final_ref[...])


def _resident(shape):
    zeros = (0,) * len(shape)
    return pl.BlockSpec(shape, lambda i: zeros, pipeline_mode=pl.Buffered(1))


def _whole(shape):
    zeros = (0,) * len(shape)
    return pl.BlockSpec(shape, lambda i: zeros)


def _rel_vector(rel_bias):
    assert CHUNK - 1 <= MAX_REL <= WIN_KEYS - CHUNK - 1
    n_clipped = WIN_KEYS - CHUNK - MAX_REL
    far = jnp.broadcast_to(rel_bias[:, 2 * MAX_REL:], (ATTN_HEADS, n_clipped))
    near = rel_bias[:, MAX_REL - (CHUNK - 1):2 * MAX_REL][:, ::-1]
    unused = jnp.zeros((ATTN_HEADS, 1), rel_bias.dtype)
    return jnp.concatenate([unused, far, near], axis=1).astype(jnp.float32)


def _block_diag(w):
    g, c, _ = w.shape
    eye = jnp.eye(g, dtype=w.dtype)
    return (eye[:, None, :, None] * w[:, :, None, :]).reshape(g * c, g * c)


def kernel(x, mem, norm_mix_g, norm_mem_g, w_in, b_gate, w_pool, pool_scale, rel_bias, w_mem_kv,
           w_up_pool, w_up_attn, w_up_mem, w_out, norm_ffn_g, w_ffn_up, conv_w, conv_b, w_ffn_down,
           norm_final_g):
    batch, seq, d = x.shape
    assert batch == 1 and d == D_MODEL and seq % SEQ_TILE == 0
    assert norm_mix_g.shape[0] == 1, "single layer"
    bf16 = jnp.bfloat16
    f32 = jnp.float32
    ts = SEQ_TILE
    n_tiles = seq // ts
    mem_len = mem.shape[1]
    params = pltpu.CompilerParams(dimension_semantics=("arbitrary",), vmem_limit_bytes=VMEM_LIMIT)

    small_w = (w_up_pool[0], w_up_attn[0], w_up_mem[0], w_out[0], _block_diag(w_pool[0]))
    prep_in = (mem[0], norm_mem_g.astype(f32), w_mem_kv[0])
    slab = pl.BlockSpec((D_MODEL // PREP_STEPS, w_in.shape[2]), lambda i: (i, 0))
    prep_out = pl.pallas_call(
        _prep_kernel,
        grid=(PREP_STEPS,),
        in_specs=[_resident(a.shape) for a in prep_in] + [slab] + [_resident(a.shape) for a in small_w],
        out_specs=[_whole((mem_len, MEM_WIDTH)), _whole((MEM_HEADS // 2, mem_len, 2 * PAIR)), slab]
                  + [_whole(a.shape) for a in small_w],
        out_shape=[jax.ShapeDtypeStruct((mem_len, MEM_WIDTH), bf16),
                   jax.ShapeDtypeStruct((MEM_HEADS // 2, mem_len, 2 * PAIR), bf16),
                   jax.ShapeDtypeStruct(w_in.shape[1:], bf16)]
                  + [jax.ShapeDtypeStruct(a.shape, bf16) for a in small_w],
        compiler_params=params,
        name="prep",
    )(*prep_in, w_in[0], *small_w)
    km, vm, w_in_bf16, w_up_pool_bf16, w_up_attn_bf16, w_up_mem_bf16, w_out_bf16, w_pool_bf16 = prep_out

    tile_spec = pl.BlockSpec((ts, D_MODEL), lambda i: (i, 0))
    mixer_inputs = (
        norm_mix_g.astype(f32), w_in_bf16, b_gate.astype(f32), w_pool_bf16,
        pool_scale.astype(f32), _rel_vector(rel_bias[0]), km, vm,
        w_up_pool_bf16, w_up_attn_bf16, w_up_mem_bf16, w_out_bf16,
    )
    ffn_w = (w_ffn_up[0], w_ffn_down[0])
    slab_steps = (1, 2)
    slab_specs = [pl.BlockSpec((w.shape[0] * k // n_tiles, w.shape[1]), functools.partial(lambda i, k: (i // k, 0), k=k))
                  for w, k in zip(ffn_w, slab_steps)]
    assert all((w.shape[0] * k) % (n_tiles * BF16_ROWS) == 0 for w, k in zip(ffn_w, slab_steps))
    x1, w_up_bf16, w_down_bf16 = pl.pallas_call(
        _mixer_kernel,
        grid=(n_tiles,),
        in_specs=[pl.BlockSpec(memory_space=pltpu.SMEM), tile_spec] + [_resident(a.shape) for a in mixer_inputs] + slab_specs,
        out_specs=[tile_spec] + slab_specs,
        out_shape=[jax.ShapeDtypeStruct((seq, D_MODEL), f32)] + [jax.ShapeDtypeStruct(w.shape, bf16) for w in ffn_w],
        scratch_shapes=[
            pltpu.VMEM((ts, D_MODEL), bf16),
            pltpu.VMEM((POOL_TAIL + ts, POOL_WIDTH), f32),
            pltpu.VMEM((2, ts, ATTN_WIDTH), bf16),
            pltpu.VMEM((KV_CARRY + ts, ATTN_WIDTH), bf16),
            pltpu.VMEM((KV_CARRY + ts, 2 * ATTN_WIDTH), bf16),
            pltpu.VMEM((2, ts, MEM_WIDTH), bf16),
            pltpu.VMEM((ts, POOL_WIDTH), bf16),
            pltpu.VMEM((ts, ATTN_WIDTH), bf16),
            pltpu.VMEM((ts, MEM_WIDTH), bf16),
            pltpu.VMEM((ts, D_MODEL), bf16),
            pltpu.VMEM((ts, D_MODEL), f32),
            pltpu.VMEM((ts // MEM_ROWS, MEM_HEADS // 2, 2 * MEM_ROWS, mem_len), bf16),
            pltpu.VMEM((ATTN_HEADS // 2, 4 * CHUNK, WIN_KEYS), f32),
            pltpu.VMEM((ATTN_HEADS // 2, 4 * CHUNK, WIN_KEYS), f32),
            pltpu.VMEM((ATTN_HEADS // 2, 4 * CHUNK, WIN_KEYS), f32),
            pltpu.VMEM((ATTN_HEADS // 2, 4 * CHUNK, LANES), f32),
            pltpu.VMEM((ATTN_HEADS // 2, 4 * CHUNK, LANES), f32),
            pltpu.VMEM((ATTN_HEADS // 2, 4 * CHUNK, WIN_KEYS), bf16),
            pltpu.VMEM((ATTN_HEADS // 2, 4 * CHUNK, WIN_KEYS), bf16),
        ],
        compiler_params=params,
        name="mixer",
    )(jnp.zeros((1,), jnp.int32), x[0], *mixer_inputs, *ffn_w)

    ffn_inputs = (
        norm_ffn_g.astype(f32), w_up_bf16, conv_w[0].astype(f32), conv_b.astype(f32),
        w_down_bf16, norm_final_g.reshape(1, -1).astype(f32),
    )
    tf = FFN_TILE
    ffn_tile_spec = pl.BlockSpec((tf, D_MODEL), lambda i: (i, 0))
    out = pl.pallas_call(
        _ffn_kernel,
        grid=(seq // tf,),
        in_specs=[ffn_tile_spec] + [_resident(a.shape) for a in ffn_inputs],
        out_specs=ffn_tile_spec,
        out_shape=jax.ShapeDtypeStruct((seq, D_MODEL), x.dtype),
        scratch_shapes=[
            pltpu.VMEM((tf, D_MODEL), bf16),
            pltpu.VMEM((CONV_PAD + FFN_SUB, FF_BLOCK), f32),
            pltpu.VMEM((CONV_PAD, 2 * D_FF), f32),
            pltpu.VMEM((tf, D_FF), bf16),
        ],
        compiler_params=params,
        name="ffn",
    )(x1, *ffn_inputs)
    return out[None]
```

```python
import functools

import jax
import jax.numpy as jnp
from jax import lax
from jax.experimental import pallas as pl
from jax.experimental.pallas import tpu as pltpu

D_MODEL = 1024
CHUNK = 64
HEAD_DIM = 64
POOL_WINDOWS = (2, 4, 8, 16)
POOL_GROUP_DIM = 64
POOL_WIDTH = len(POOL_WINDOWS) * POOL_GROUP_DIM
ATTN_HEADS = 8
ATTN_WIDTH = ATTN_HEADS * HEAD_DIM
BAND_CHUNKS = 9
BAND_KEYS = BAND_CHUNKS * CHUNK
MAX_REL = 128
MEM_HEADS = 4
MEM_WIDTH = MEM_HEADS * HEAD_DIM
N_BRANCH = 3
D_FF = 2816
CONV_WIDTH = 3
RMS_EPS = 1e-6
NEG_INF = -1e30

O_POOL = 0
O_Q = O_POOL + POOL_WIDTH
O_K = O_Q + ATTN_WIDTH
O_V = O_K + ATTN_WIDTH
O_QM = O_V + ATTN_WIDTH
O_GATE = O_QM + MEM_WIDTH

LANES = 128
SUBLANES = 8
BF16_ROWS = 2 * SUBLANES
PAIR = 2 * HEAD_DIM
POOL_TAIL = 16
WIN_KEYS = BAND_KEYS + CHUNK
KV_CARRY = BAND_KEYS - CHUNK
SEQ_TILE = 512
FFN_TILE = 1024
FFN_SUB = 1024
MEM_ROWS = 128
PREP_STEPS = 8
FF_BLOCK = 256
CONV_PAD = SUBLANES
VMEM_LIMIT = 60 * 1024 * 1024


def _rms_norm(x, g):
    y = x * lax.rsqrt(jnp.mean(x * x, axis=-1, keepdims=True) + RMS_EPS)
    return y * g


def _dot(a, b):
    return jnp.dot(a, b, preferred_element_type=jnp.float32)


def _dot_nt(a, b):
    return lax.dot_general(a, b, (((1,), (1,)), ((), ())), preferred_element_type=jnp.float32)


def _prep_kernel(mem_ref, g_ref, w_kv_ref, *refs):
    n_w = (len(refs) - 2) // 2
    k_ref, v_ref = refs[n_w:n_w + 2]
    for src, dst in zip(refs[:n_w], refs[n_w + 2:]):
        dst[...] = src[...].astype(jnp.bfloat16)

    @pl.when(pl.program_id(0) == 0)
    def _first():
        mem_len = mem_ref.shape[0]
        mem_n = _rms_norm(mem_ref[...], g_ref[...]).astype(jnp.bfloat16)
        kv = _dot(mem_n, w_kv_ref[...].astype(jnp.bfloat16))
        k_ref[...] = kv[:, :MEM_WIDTH].astype(jnp.bfloat16)
        for p in range(MEM_HEADS // 2):
            v_ref[p, :, 0:PAIR] = kv[:, MEM_WIDTH + p * PAIR:MEM_WIDTH + (p + 1) * PAIR].astype(jnp.bfloat16)
            v_ref[p, :, PAIR:2 * PAIR] = jnp.ones((mem_len, PAIR), jnp.bfloat16)


def _mixer_kernel(zero_ref, x_ref, g_ref, w_in_ref, b_gate_ref, w_pool_ref, pool_scale_ref, rel_vec_ref,
                  km_ref, vm_ref, w_up_pool_ref, w_up_attn_ref, w_up_mem_ref, w_out_ref,
                  w_ffn_up_ref, w_ffn_down_ref,
                  o_ref, w_ffn_up_bf16_ref, w_ffn_down_bf16_ref,
                  h_buf, u_buf, q_buf, k_buf, v_buf, qm_buf, pool_buf, attn_buf, mem_buf, merged_buf,
                  part_buf, pm_buf, bias_buf, s_buf0, s_buf1, m_buf0, m_buf1, p_buf0, p_buf1):
    s_buf, m_buf, p_buf = (s_buf0, s_buf1), (m_buf0, m_buf1), (p_buf0, p_buf1)
    ts = x_ref.shape[0]
    tile = pl.program_id(0)

    @pl.when(tile == 0)
    def _init():
        u_buf[0:POOL_TAIL, :] = jnp.zeros((POOL_TAIL, POOL_WIDTH), jnp.float32)
        k_buf[0:KV_CARRY, :] = jnp.zeros((KV_CARRY, ATTN_WIDTH), jnp.bfloat16)
        ext_lane = lax.broadcasted_iota(jnp.int32, v_buf.shape, 1) % (2 * PAIR)
        v_buf[...] = jnp.where(ext_lane >= PAIR, 1.0, 0.0).astype(jnp.bfloat16)
        win_lane = lax.broadcasted_iota(jnp.int32, (ATTN_HEADS, WIN_KEYS), 1)
        vecs = rel_vec_ref[...]
        for qi in range(CHUNK):
            first = jnp.where(win_lane >= BAND_KEYS, NEG_INF, pltpu.roll(vecs, qi + BAND_KEYS, axis=1))
            second = jnp.where(win_lane < CHUNK, NEG_INF, pltpu.roll(vecs, qi, axis=1) if qi else vecs)
            for h in range(ATTN_HEADS):
                r = (h % 2) * 2 * CHUNK + qi
                bias_buf[h // 2, r:r + 1, :] = first[h:h + 1, :]
                bias_buf[h // 2, r + CHUNK:r + CHUNK + 1, :] = second[h:h + 1, :]

    w_ffn_up_bf16_ref[...] = w_ffn_up_ref[...].astype(jnp.bfloat16)
    w_ffn_down_bf16_ref[...] = w_ffn_down_ref[...].astype(jnp.bfloat16)

    h_buf[...] = _rms_norm(x_ref[...], g_ref[...]).astype(jnp.bfloat16)

    u_buf[POOL_TAIL:POOL_TAIL + ts, :] = _dot(h_buf[...], w_in_ref[:, O_POOL:O_Q])
    def split_heads(q, even_ref, odd_ref):
        odd = (lax.broadcasted_iota(jnp.int32, q.shape, 1) // HEAD_DIM) % 2 == 1
        q = q * (HEAD_DIM ** -0.5)
        even_ref[...] = jnp.where(odd, 0.0, q).astype(jnp.bfloat16)
        odd_ref[...] = jnp.where(odd, q, 0.0).astype(jnp.bfloat16)

    split_heads(_dot(h_buf[...], w_in_ref[:, O_Q:O_K]), q_buf.at[0], q_buf.at[1])
    k_buf[KV_CARRY:KV_CARRY + ts, :] = _dot(h_buf[...], w_in_ref[:, O_K:O_V]).astype(jnp.bfloat16)
    v = _dot(h_buf[...], w_in_ref[:, O_V:O_QM]).astype(jnp.bfloat16)
    for p in range(ATTN_HEADS // 2):
        v_buf[KV_CARRY:KV_CARRY + ts, p * 2 * PAIR:p * 2 * PAIR + PAIR] = v[:, p * PAIR:(p + 1) * PAIR]
    split_heads(_dot(h_buf[...], w_in_ref[:, O_QM:O_GATE]), qm_buf.at[0], qm_buf.at[1])

    u = u_buf[POOL_TAIL:POOL_TAIL + ts, :]
    lane_group = lax.broadcasted_iota(jnp.int32, (ts, POOL_WIDTH), 1) // POOL_GROUP_DIM
    run = u
    win_sum = jnp.zeros_like(u)
    for j in range(1, POOL_TAIL + 1):
        if j in POOL_WINDOWS:
            win_sum = jnp.where(lane_group == POOL_WINDOWS.index(j), run, win_sum)
        if j < POOL_TAIL:
            run = run + u_buf[POOL_TAIL - j:POOL_TAIL - j + ts, :]
    window = jnp.left_shift(2, lane_group)
    pos1 = tile * ts + lax.broadcasted_iota(jnp.int32, (ts, POOL_WIDTH), 0) + 1
    cnt = jnp.minimum(pos1, window).astype(jnp.float32)
    pooled = (win_sum / cnt - u).astype(jnp.bfloat16)
    pool_buf[...] = (_dot(pooled, w_pool_ref[...]) * pool_scale_ref[...]).astype(jnp.bfloat16)
    u_buf[0:POOL_TAIL, :] = u_buf[ts:ts + POOL_TAIL, :]

    def pair_cols(h):
        return slice((h // 2) * PAIR, (h // 2 + 1) * PAIR)

    staged = zero_ref[0]

    def pair_output(acc):
        half = acc.shape[0] // 2
        out = acc[:, :PAIR] / acc[:, PAIR:]
        lane_is_odd = lax.broadcasted_iota(jnp.int32, (half, PAIR), 1) >= HEAD_DIM
        return jnp.where(lane_is_odd, out[half:], out[:half]).astype(jnp.bfloat16)

    mem_len = km_ref.shape[0]
    n_mem = ts // MEM_ROWS

    def mem_score(r):
        r0 = r * MEM_ROWS
        for h in range(MEM_HEADS):
            s = _dot_nt(qm_buf[h % 2, r0:r0 + MEM_ROWS, pair_cols(h)], km_ref[:, pair_cols(h)])
            e = jnp.exp(s - jnp.max(s, axis=-1, keepdims=True))
            pm_buf[r, h // 2, (h % 2) * MEM_ROWS:(h % 2 + 1) * MEM_ROWS, :] = e.astype(jnp.bfloat16)

    def mem_value(r):
        r0 = r * MEM_ROWS
        for p in range(MEM_HEADS // 2):
            acc = _dot(pm_buf[staged + r, p], vm_ref[p])
            mem_buf[r0:r0 + MEM_ROWS, pair_cols(2 * p)] = pair_output(acc)

    merge_blk = 2 * LANES
    n_merge = D_MODEL // merge_blk

    def merge_cols(nb):
        return slice(nb * merge_blk, (nb + 1) * merge_blk)

    def gated(b, branch_buf, w_up_ref, nb):
        lo = b * D_MODEL + nb * merge_blk
        logits = _dot(h_buf[...], w_in_ref[:, O_GATE + lo:O_GATE + lo + merge_blk]) + b_gate_ref[:, lo:lo + merge_blk]
        return jax.nn.sigmoid(logits) * _dot(branch_buf[...], w_up_ref[:, merge_cols(nb)])

    win_col = lax.broadcasted_iota(jnp.int32, (1, WIN_KEYS), 1)
    pad_rows = jnp.where(tile == 0, KV_CARRY, 0)

    def ext_cols(p):
        return slice(p * 2 * PAIR, (p + 1) * 2 * PAIR)

    n_k = WIN_KEYS // LANES
    n_cp = ts // (2 * CHUNK)

    def score_phase(cp):
        base = cp * 2 * CHUNK
        slot = cp % 2
        pen = jnp.where(win_col + base < pad_rows, NEG_INF, 0.0)
        for p in range(ATTN_HEADS // 2):
            q_rows = jnp.concatenate([q_buf[0, base:base + 2 * CHUNK, pair_cols(2 * p)],
                                      q_buf[1, base:base + 2 * CHUNK, pair_cols(2 * p)]], axis=0)
            s = _dot_nt(q_rows, k_buf[base:base + WIN_KEYS, pair_cols(2 * p)]) + bias_buf[p] + pen
            s_buf[slot][p] = s
            part = s[:, 0:LANES]
            for k in range(1, n_k):
                part = jnp.maximum(part, s[:, k * LANES:(k + 1) * LANES])
            m_buf[slot][p] = jnp.broadcast_to(jnp.max(part, axis=-1, keepdims=True), (4 * CHUNK, LANES))

    def softmax_phase(cp):
        slot = cp % 2
        for p in range(ATTN_HEADS // 2):
            m = m_buf[slot][staged + p]
            for k in range(n_k):
                cols = slice(k * LANES, (k + 1) * LANES)
                p_buf[slot][p, :, cols] = jnp.exp(s_buf[slot][staged + p, :, cols] - m).astype(jnp.bfloat16)

    def value_phase(cp):
        base = cp * 2 * CHUNK
        slot = cp % 2
        for p in range(ATTN_HEADS // 2):
            acc = _dot(p_buf[slot][staged + p], v_buf[base:base + WIN_KEYS, ext_cols(p)])
            attn_buf[base:base + 2 * CHUNK, pair_cols(2 * p)] = pair_output(acc)

    mem_steps = 2
    mem_per_step = n_mem // mem_steps
    assert mem_steps + n_merge <= n_cp + 2
    for step in range(n_cp + 2):
        mem_blocks = range(step * mem_per_step, (step + 1) * mem_per_step) if step < mem_steps else ()
        for r in mem_blocks:
            mem_score(r)
        if step < n_cp:
            score_phase(step)
        if 0 <= step - mem_steps < n_merge:
            nb = step - mem_steps
            part_buf[:, merge_cols(nb)] = (gated(0, pool_buf, w_up_pool_ref, nb)
                                           + gated(2, mem_buf, w_up_mem_ref, nb))
        if 0 <= step - 1 < n_cp:
            softmax_phase(step - 1)
        if 0 <= step - 2 < n_cp:
            value_phase(step - 2)
        for r in mem_blocks:
            mem_value(r)

    k_buf[0:KV_CARRY, :] = k_buf[ts:ts + KV_CARRY, :]
    for p in range(ATTN_HEADS // 2):
        vcols = slice(p * 2 * PAIR, p * 2 * PAIR + PAIR)
        v_buf[0:KV_CARRY, vcols] = v_buf[ts:ts + KV_CARRY, vcols]

    for nb in range(n_merge):
        merged = part_buf[:, merge_cols(nb)] + gated(1, attn_buf, w_up_attn_ref, nb)
        merged_buf[:, merge_cols(nb)] = merged.astype(jnp.bfloat16)
    o_ref[...] = x_ref[...] + _dot(merged_buf[...], w_out_ref[...])


def _ffn_kernel(x_ref, g_ref, w_up_ref, conv_w_ref, conv_b_ref, w_down_ref, g_final_ref, o_ref,
                h_buf, a_buf, tail_buf, hid_buf):
    sub = a_buf.shape[0] - CONV_PAD

    @pl.when(pl.program_id(0) == 0)
    def _init():
        tail_buf[...] = jnp.zeros_like(tail_buf)

    def conv_up(rows, cols):
        a = _dot(h_buf[rows, :], w_up_ref[:, cols])
        a_buf[0:CONV_PAD, :] = tail_buf[:, cols]
        a_buf[CONV_PAD:CONV_PAD + sub, :] = a
        tail_buf[:, cols] = a_buf[sub:sub + CONV_PAD, :]
        out = conv_b_ref[:, cols] + a * conv_w_ref[CONV_WIDTH - 1:CONV_WIDTH, cols]
        for t in range(CONV_WIDTH - 1):
            back = CONV_WIDTH - 1 - t
            out = out + a_buf[CONV_PAD - back:CONV_PAD - back + sub, :] * conv_w_ref[t:t + 1, cols]
        return out

    for r0 in range(0, x_ref.shape[0], sub):
        rows = slice(r0, r0 + sub)
        h_buf[rows, :] = _rms_norm(x_ref[rows, :], g_ref[...]).astype(jnp.bfloat16)
        for jb in range(D_FF // FF_BLOCK):
            gate = conv_up(rows, slice(jb * FF_BLOCK, (jb + 1) * FF_BLOCK))
            val = conv_up(rows, slice(D_FF + jb * FF_BLOCK, D_FF + (jb + 1) * FF_BLOCK))
            act = 0.5 * gate * (1.0 + lax.erf(gate * (2.0 ** -0.5)))
            hid_buf[rows, jb * FF_BLOCK:(jb + 1) * FF_BLOCK] = (act * val).astype(jnp.bfloat16)
        for q0 in range(r0, r0 + sub, sub // 4):
            quarter = slice(q0, q0 + sub // 4)
            y = x_ref[quarter, :] + _dot(hid_buf[quarter, :], w_down_ref[...])
            o_ref[quarter, :] = _rms_norm(y, g_final_ref[...])


def _resident(shape):
    zeros = (0,) * len(shape)
    return pl.BlockSpec(shape, lambda i: zeros, pipeline_mode=pl.Buffered(1))


def _whole(shape):
    zeros = (0,) * len(shape)
    return pl.BlockSpec(shape, lambda i: zeros)


def _rel_vector(rel_bias):
    assert CHUNK - 1 <= MAX_REL <= WIN_KEYS - CHUNK - 1
    n_clipped = WIN_KEYS - CHUNK - MAX_REL
    far = jnp.broadcast_to(rel_bias[:, 2 * MAX_REL:], (ATTN_HEADS, n_clipped))
    near = rel_bias[:, MAX_REL - (CHUNK - 1):2 * MAX_REL][:, ::-1]
    unused = jnp.zeros((ATTN_HEADS, 1), rel_bias.dtype)
    return jnp.concatenate([unused, far, near], axis=1).astype(jnp.float32)


def _block_diag(w):
    g, c, _ = w.shape
    eye = jnp.eye(g, dtype=w.dtype)
    return (eye[:, None, :, None] * w[:, :, None, :]).reshape(g * c, g * c)


def kernel(x, mem, norm_mix_g, norm_mem_g, w_in, b_gate, w_pool, pool_scale, rel_bias, w_mem_kv,
           w_up_pool, w_up_attn, w_up_mem, w_out, norm_ffn_g, w_ffn_up, conv_w, conv_b, w_ffn_down,
           norm_final_g):
    batch, seq, d = x.shape
    assert batch == 1 and d == D_MODEL and seq % SEQ_TILE == 0
    assert norm_mix_g.shape[0] == 1, "single layer"
    bf16 = jnp.bfloat16
    f32 = jnp.float32
    ts = SEQ_TILE
    n_tiles = seq // ts
    mem_len = mem.shape[1]
    params = pltpu.CompilerParams(dimension_semantics=("arbitrary",), vmem_limit_bytes=VMEM_LIMIT)

    mixer_w = (w_in[0], w_up_pool[0], w_up_attn[0], w_up_mem[0], w_out[0], _block_diag(w_pool[0]))
    prep_in = (mem[0], norm_mem_g.astype(f32), w_mem_kv[0])
    assert all(w.shape[0] % (PREP_STEPS * BF16_ROWS) == 0 for w in mixer_w)
    w_slabs = [pl.BlockSpec((w.shape[0] // PREP_STEPS, w.shape[1]), lambda i: (i, 0)) for w in mixer_w]
    prep_out = pl.pallas_call(
        _prep_kernel,
        grid=(PREP_STEPS,),
        in_specs=[_resident(a.shape) for a in prep_in] + w_slabs,
        out_specs=[_whole((mem_len, MEM_WIDTH)), _whole((MEM_HEADS // 2, mem_len, 2 * PAIR))] + w_slabs,
        out_shape=[jax.ShapeDtypeStruct((mem_len, MEM_WIDTH), bf16),
                   jax.ShapeDtypeStruct((MEM_HEADS // 2, mem_len, 2 * PAIR), bf16)]
                  + [jax.ShapeDtypeStruct(w.shape, bf16) for w in mixer_w],
        compiler_params=params,
        name="prep",
    )(*prep_in, *mixer_w)
    km, vm, w_in_bf16, w_up_pool_bf16, w_up_attn_bf16, w_up_mem_bf16, w_out_bf16, w_pool_bf16 = prep_out

    tile_spec = pl.BlockSpec((ts, D_MODEL), lambda i: (i, 0))
    mixer_inputs = (
        norm_mix_g.astype(f32), w_in_bf16, b_gate.astype(f32), w_pool_bf16,
        pool_scale.astype(f32), _rel_vector(rel_bias[0]), km, vm,
        w_up_pool_bf16, w_up_attn_bf16, w_up_mem_bf16, w_out_bf16,
    )
    ffn_w = (w_ffn_up[0], w_ffn_down[0])
    slab_steps = (1, 2)
    slab_specs = [pl.BlockSpec((w.shape[0] * k // n_tiles, w.shape[1]), functools.partial(lambda i, k: (i // k, 0), k=k))
                  for w, k in zip(ffn_w, slab_steps)]
    assert all((w.shape[0] * k) % (n_tiles * BF16_ROWS) == 0 for w, k in zip(ffn_w, slab_steps))
    x1, w_up_bf16, w_down_bf16 = pl.pallas_call(
        _mixer_kernel,
        grid=(n_tiles,),
        in_specs=[pl.BlockSpec(memory_space=pltpu.SMEM), tile_spec] + [_resident(a.shape) for a in mixer_inputs] + slab_specs,
        out_specs=[tile_spec] + slab_specs,
        out_shape=[jax.ShapeDtypeStruct((seq, D_MODEL), f32)] + [jax.ShapeDtypeStruct(w.shape, bf16) for w in ffn_w],
        scratch_shapes=[
            pltpu.VMEM((ts, D_MODEL), bf16),
            pltpu.VMEM((POOL_TAIL + ts, POOL_WIDTH), f32),
            pltpu.VMEM((2, ts, ATTN_WIDTH), bf16),
            pltpu.VMEM((KV_CARRY + ts, ATTN_WIDTH), bf16),
            pltpu.VMEM((KV_CARRY + ts, 2 * ATTN_WIDTH), bf16),
            pltpu.VMEM((2, ts, MEM_WIDTH), bf16),
            pltpu.VMEM((ts, POOL_WIDTH), bf16),
            pltpu.VMEM((ts, ATTN_WIDTH), bf16),
            pltpu.VMEM((ts, MEM_WIDTH), bf16),
            pltpu.VMEM((ts, D_MODEL), bf16),
            pltpu.VMEM((ts, D_MODEL), f32),
            pltpu.VMEM((ts // MEM_ROWS, MEM_HEADS // 2, 2 * MEM_ROWS, mem_len), bf16),
            pltpu.VMEM((ATTN_HEADS // 2, 4 * CHUNK, WIN_KEYS), f32),
            pltpu.VMEM((ATTN_HEADS // 2, 4 * CHUNK, WIN_KEYS), f32),
            pltpu.VMEM((ATTN_HEADS // 2, 4 * CHUNK, WIN_KEYS), f32),
            pltpu.VMEM((ATTN_HEADS // 2, 4 * CHUNK, LANES), f32),
            pltpu.VMEM((ATTN_HEADS // 2, 4 * CHUNK, LANES), f32),
            pltpu.VMEM((ATTN_HEADS // 2, 4 * CHUNK, WIN_KEYS), bf16),
            pltpu.VMEM((ATTN_HEADS // 2, 4 * CHUNK, WIN_KEYS), bf16),
        ],
        compiler_params=params,
        name="mixer",
    )(jnp.zeros((1,), jnp.int32), x[0], *mixer_inputs, *ffn_w)

    ffn_inputs = (
        norm_ffn_g.astype(f32), w_up_bf16, conv_w[0].astype(f32), conv_b.astype(f32),
        w_down_bf16, norm_final_g.reshape(1, -1).astype(f32),
    )
    tf = FFN_TILE
    ffn_tile_spec = pl.BlockSpec((tf, D_MODEL), lambda i: (i, 0))
    out = pl.pallas_call(
        _ffn_kernel,
        grid=(seq // tf,),
        in_specs=[ffn_tile_spec] + [_resident(a.shape) for a in ffn_inputs],
        out_specs=ffn_tile_spec,
        out_shape=jax.ShapeDtypeStruct((seq, D_MODEL), x.dtype),
        scratch_shapes=[
            pltpu.VMEM((tf, D_MODEL), bf16),
            pltpu.VMEM((CONV_PAD + FFN_SUB, FF_BLOCK), f32),
            pltpu.VMEM((CONV_PAD, 2 * D_FF), f32),
            pltpu.VMEM((tf, D_FF), bf16),
        ],
        compiler_params=params,
        name="ffn",
    )(x1, *ffn_inputs)
    return out[None]
```

```python
import functools

import jax
import jax.numpy as jnp
from jax import lax
from jax.experimental import pallas as pl
from jax.experimental.pallas import tpu as pltpu

D_MODEL = 1024
CHUNK = 64
HEAD_DIM = 64
POOL_WINDOWS = (2, 4, 8, 16)
POOL_GROUP_DIM = 64
POOL_WIDTH = len(POOL_WINDOWS) * POOL_GROUP_DIM
ATTN_HEADS = 8
ATTN_WIDTH = ATTN_HEADS * HEAD_DIM
BAND_CHUNKS = 9
BAND_KEYS = BAND_CHUNKS * CHUNK
MAX_REL = 128
MEM_HEADS = 4
MEM_WIDTH = MEM_HEADS * HEAD_DIM
N_BRANCH = 3
D_FF = 2816
CONV_WIDTH = 3
RMS_EPS = 1e-6
NEG_INF = -1e30

O_POOL = 0
O_Q = O_POOL + POOL_WIDTH
O_K = O_Q + ATTN_WIDTH
O_V = O_K + ATTN_WIDTH
O_QM = O_V + ATTN_WIDTH
O_GATE = O_QM + MEM_WIDTH

LANES = 128
SUBLANES = 8
BF16_ROWS = 2 * SUBLANES
PAIR = 2 * HEAD_DIM
POOL_TAIL = 16
WIN_KEYS = BAND_KEYS + CHUNK
KV_CARRY = BAND_KEYS - CHUNK
SEQ_TILE = 512
FFN_TILE = 1024
FFN_SUB = 1024
MEM_ROWS = 256
PREP_STEPS = 8
FF_BLOCK = 256
CONV_PAD = SUBLANES
VMEM_LIMIT = 60 * 1024 * 1024


def _rms_norm(x, g):
    y = x * lax.rsqrt(jnp.mean(x * x, axis=-1, keepdims=True) + RMS_EPS)
    return y * g


def _dot(a, b):
    return jnp.dot(a, b, preferred_element_type=jnp.float32)


def _dot_nt(a, b):
    return lax.dot_general(a, b, (((1,), (1,)), ((), ())), preferred_element_type=jnp.float32)


def _prep_kernel(mem_ref, g_ref, w_kv_ref, *refs):
    n_w = (len(refs) - 2) // 2
    k_ref, v_ref = refs[n_w:n_w + 2]
    for src, dst in zip(refs[:n_w], refs[n_w + 2:]):
        dst[...] = src[...].astype(jnp.bfloat16)

    @pl.when(pl.program_id(0) == 0)
    def _first():
        mem_len = mem_ref.shape[0]
        mem_n = _rms_norm(mem_ref[...], g_ref[...]).astype(jnp.bfloat16)
        kv = _dot(mem_n, w_kv_ref[...].astype(jnp.bfloat16))
        k_ref[...] = kv[:, :MEM_WIDTH].astype(jnp.bfloat16)
        for p in range(MEM_HEADS // 2):
            v_ref[p, :, 0:PAIR] = kv[:, MEM_WIDTH + p * PAIR:MEM_WIDTH + (p + 1) * PAIR].astype(jnp.bfloat16)
            v_ref[p, :, PAIR:2 * PAIR] = jnp.ones((mem_len, PAIR), jnp.bfloat16)


def _mixer_kernel(zero_ref, x_ref, g_ref, w_in_ref, b_gate_ref, w_pool_ref, pool_scale_ref, rel_vec_ref,
                  km_ref, vm_ref, w_up_pool_ref, w_up_attn_ref, w_up_mem_ref, w_out_ref,
                  w_ffn_up_ref, w_ffn_down_ref,
                  o_ref, w_ffn_up_bf16_ref, w_ffn_down_bf16_ref,
                  h_buf, u_buf, q_buf, k_buf, v_buf, qm_buf, pool_buf, attn_buf, mem_buf, merged_buf,
                  part_buf, pm_buf, bias_buf, s_buf0, s_buf1, m_buf0, m_buf1, p_buf0, p_buf1):
    s_buf, m_buf, p_buf = (s_buf0, s_buf1), (m_buf0, m_buf1), (p_buf0, p_buf1)
    ts = x_ref.shape[0]
    tile = pl.program_id(0)

    @pl.when(tile == 0)
    def _init():
        u_buf[0:POOL_TAIL, :] = jnp.zeros((POOL_TAIL, POOL_WIDTH), jnp.float32)
        k_buf[0:KV_CARRY, :] = jnp.zeros((KV_CARRY, ATTN_WIDTH), jnp.bfloat16)
        ext_lane = lax.broadcasted_iota(jnp.int32, v_buf.shape, 1) % (2 * PAIR)
        v_buf[...] = jnp.where(ext_lane >= PAIR, 1.0, 0.0).astype(jnp.bfloat16)
        win_lane = lax.broadcasted_iota(jnp.int32, (ATTN_HEADS, WIN_KEYS), 1)
        vecs = rel_vec_ref[...]
        for qi in range(CHUNK):
            first = jnp.where(win_lane >= BAND_KEYS, NEG_INF, pltpu.roll(vecs, qi + BAND_KEYS, axis=1))
            second = jnp.where(win_lane < CHUNK, NEG_INF, pltpu.roll(vecs, qi, axis=1) if qi else vecs)
            for h in range(ATTN_HEADS):
                r = (h % 2) * 2 * CHUNK + qi
                bias_buf[h // 2, r:r + 1, :] = first[h:h + 1, :]
                bias_buf[h // 2, r + CHUNK:r + CHUNK + 1, :] = second[h:h + 1, :]

    w_ffn_up_bf16_ref[...] = w_ffn_up_ref[...].astype(jnp.bfloat16)
    w_ffn_down_bf16_ref[...] = w_ffn_down_ref[...].astype(jnp.bfloat16)

    h_buf[...] = _rms_norm(x_ref[...], g_ref[...]).astype(jnp.bfloat16)

    u_buf[POOL_TAIL:POOL_TAIL + ts, :] = _dot(h_buf[...], w_in_ref[:, O_POOL:O_Q])
    def split_heads(q, even_ref, odd_ref):
        odd = (lax.broadcasted_iota(jnp.int32, q.shape, 1) // HEAD_DIM) % 2 == 1
        q = q * (HEAD_DIM ** -0.5)
        even_ref[...] = jnp.where(odd, 0.0, q).astype(jnp.bfloat16)
        odd_ref[...] = jnp.where(odd, q, 0.0).astype(jnp.bfloat16)

    split_heads(_dot(h_buf[...], w_in_ref[:, O_Q:O_K]), q_buf.at[0], q_buf.at[1])
    k_buf[KV_CARRY:KV_CARRY + ts, :] = _dot(h_buf[...], w_in_ref[:, O_K:O_V]).astype(jnp.bfloat16)
    v = _dot(h_buf[...], w_in_ref[:, O_V:O_QM]).astype(jnp.bfloat16)
    for p in range(ATTN_HEADS // 2):
        v_buf[KV_CARRY:KV_CARRY + ts, p * 2 * PAIR:p * 2 * PAIR + PAIR] = v[:, p * PAIR:(p + 1) * PAIR]
    split_heads(_dot(h_buf[...], w_in_ref[:, O_QM:O_GATE]), qm_buf.at[0], qm_buf.at[1])

    u = u_buf[POOL_TAIL:POOL_TAIL + ts, :]
    lane_group = lax.broadcasted_iota(jnp.int32, (ts, POOL_WIDTH), 1) // POOL_GROUP_DIM
    run = u
    win_sum = jnp.zeros_like(u)
    for j in range(1, POOL_TAIL + 1):
        if j in POOL_WINDOWS:
            win_sum = jnp.where(lane_group == POOL_WINDOWS.index(j), run, win_sum)
        if j < POOL_TAIL:
            run = run + u_buf[POOL_TAIL - j:POOL_TAIL - j + ts, :]
    window = jnp.left_shift(2, lane_group)
    pos1 = tile * ts + lax.broadcasted_iota(jnp.int32, (ts, POOL_WIDTH), 0) + 1
    cnt = jnp.minimum(pos1, window).astype(jnp.float32)
    pooled = (win_sum / cnt - u).astype(jnp.bfloat16)
    pool_buf[...] = (_dot(pooled, w_pool_ref[...]) * pool_scale_ref[...]).astype(jnp.bfloat16)
    u_buf[0:POOL_TAIL, :] = u_buf[ts:ts + POOL_TAIL, :]

    def pair_cols(h):
        return slice((h // 2) * PAIR, (h // 2 + 1) * PAIR)

    staged = zero_ref[0]

    def pair_output(acc):
        half = acc.shape[0] // 2
        out = acc[:, :PAIR] / acc[:, PAIR:]
        lane_is_odd = lax.broadcasted_iota(jnp.int32, (half, PAIR), 1) >= HEAD_DIM
        return jnp.where(lane_is_odd, out[half:], out[:half]).astype(jnp.bfloat16)

    mem_len = km_ref.shape[0]
    n_mem = ts // MEM_ROWS

    def mem_score(r):
        r0 = r * MEM_ROWS
        for h in range(MEM_HEADS):
            s = _dot_nt(qm_buf[h % 2, r0:r0 + MEM_ROWS, pair_cols(h)], km_ref[:, pair_cols(h)])
            e = jnp.exp(s - jnp.max(s, axis=-1, keepdims=True))
            pm_buf[r, h // 2, (h % 2) * MEM_ROWS:(h % 2 + 1) * MEM_ROWS, :] = e.astype(jnp.bfloat16)

    def mem_value(r):
        r0 = r * MEM_ROWS
        for p in range(MEM_HEADS // 2):
            acc = _dot(pm_buf[staged + r, p], vm_ref[p])
            mem_buf[r0:r0 + MEM_ROWS, pair_cols(2 * p)] = pair_output(acc)

    merge_blk = 2 * LANES
    n_merge = D_MODEL // merge_blk

    def merge_cols(nb):
        return slice(nb * merge_blk, (nb + 1) * merge_blk)

    def gated(b, branch_buf, w_up_ref, nb):
        lo = b * D_MODEL + nb * merge_blk
        logits = _dot(h_buf[...], w_in_ref[:, O_GATE + lo:O_GATE + lo + merge_blk]) + b_gate_ref[:, lo:lo + merge_blk]
        return jax.nn.sigmoid(logits) * _dot(branch_buf[...], w_up_ref[:, merge_cols(nb)])

    win_col = lax.broadcasted_iota(jnp.int32, (1, WIN_KEYS), 1)
    pad_rows = jnp.where(tile == 0, KV_CARRY, 0)

    def ext_cols(p):
        return slice(p * 2 * PAIR, (p + 1) * 2 * PAIR)

    n_k = WIN_KEYS // LANES
    n_cp = ts // (2 * CHUNK)

    def score_phase(cp):
        base = cp * 2 * CHUNK
        slot = cp % 2
        pen = jnp.where(win_col + base < pad_rows, NEG_INF, 0.0)
        for p in range(ATTN_HEADS // 2):
            q_rows = jnp.concatenate([q_buf[0, base:base + 2 * CHUNK, pair_cols(2 * p)],
                                      q_buf[1, base:base + 2 * CHUNK, pair_cols(2 * p)]], axis=0)
            s = _dot_nt(q_rows, k_buf[base:base + WIN_KEYS, pair_cols(2 * p)]) + bias_buf[p] + pen
            s_buf[slot][p] = s
            part = s[:, 0:LANES]
            for k in range(1, n_k):
                part = jnp.maximum(part, s[:, k * LANES:(k + 1) * LANES])
            m_buf[slot][p] = jnp.broadcast_to(jnp.max(part, axis=-1, keepdims=True), (4 * CHUNK, LANES))

    def softmax_phase(cp):
        slot = cp % 2
        for p in range(ATTN_HEADS // 2):
            m = m_buf[slot][staged + p]
            for k in range(n_k):
                cols = slice(k * LANES, (k + 1) * LANES)
                p_buf[slot][p, :, cols] = jnp.exp(s_buf[slot][staged + p, :, cols] - m).astype(jnp.bfloat16)

    def value_phase(cp):
        base = cp * 2 * CHUNK
        slot = cp % 2
        for p in range(ATTN_HEADS // 2):
            acc = _dot(p_buf[slot][staged + p], v_buf[base:base + WIN_KEYS, ext_cols(p)])
            attn_buf[base:base + 2 * CHUNK, pair_cols(2 * p)] = pair_output(acc)

    mem_steps = 2
    mem_per_step = n_mem // mem_steps
    assert mem_steps + n_merge <= n_cp + 2
    for step in range(n_cp + 2):
        mem_blocks = range(step * mem_per_step, (step + 1) * mem_per_step) if step < mem_steps else ()
        for r in mem_blocks:
            mem_score(r)
        if step < n_cp:
            score_phase(step)
        if 0 <= step - mem_steps < n_merge:
            nb = step - mem_steps
            part_buf[:, merge_cols(nb)] = (gated(0, pool_buf, w_up_pool_ref, nb)
                                           + gated(2, mem_buf, w_up_mem_ref, nb))
        if 0 <= step - 1 < n_cp:
            softmax_phase(step - 1)
        if 0 <= step - 2 < n_cp:
            value_phase(step - 2)
        for r in mem_blocks:
            mem_value(r)

    k_buf[0:KV_CARRY, :] = k_buf[ts:ts + KV_CARRY, :]
    for p in range(ATTN_HEADS // 2):
        vcols = slice(p * 2 * PAIR, p * 2 * PAIR + PAIR)
        v_buf[0:KV_CARRY, vcols] = v_buf[ts:ts + KV_CARRY, vcols]

    for nb in range(n_merge):
        merged = part_buf[:, merge_cols(nb)] + gated(1, attn_buf, w_up_attn_ref, nb)
        merged_buf[:, merge_cols(nb)] = merged.astype(jnp.bfloat16)
    o_ref[...] = x_ref[...] + _dot(merged_buf[...], w_out_ref[...])


def _ffn_kernel(x_ref, g_ref, w_up_ref, conv_w_ref, conv_b_ref, w_down_ref, g_final_ref, o_ref,
                h_buf, a_buf, tail_buf, hid_buf):
    sub = a_buf.shape[0] - CONV_PAD

    @pl.when(pl.program_id(0) == 0)
    def _init():
        tail_buf[...] = jnp.zeros_like(tail_buf)

    def conv_up(rows, cols):
        a = _dot(h_buf[rows, :], w_up_ref[:, cols])
        a_buf[0:CONV_PAD, :] = tail_buf[:, cols]
        a_buf[CONV_PAD:CONV_PAD + sub, :] = a
        tail_buf[:, cols] = a_buf[sub:sub + CONV_PAD, :]
        out = conv_b_ref[:, cols] + a * conv_w_ref[CONV_WIDTH - 1:CONV_WIDTH, cols]
        for t in range(CONV_WIDTH - 1):
            back = CONV_WIDTH - 1 - t
            out = out + a_buf[CONV_PAD - back:CONV_PAD - back + sub, :] * conv_w_ref[t:t + 1, cols]
        return out

    for r0 in range(0, x_ref.shape[0], sub):
        rows = slice(r0, r0 + sub)
        h_buf[rows, :] = _rms_norm(x_ref[rows, :], g_ref[...]).astype(jnp.bfloat16)
        for jb in range(D_FF // FF_BLOCK):
            gate = conv_up(rows, slice(jb * FF_BLOCK, (jb + 1) * FF_BLOCK))
            val = conv_up(rows, slice(D_FF + jb * FF_BLOCK, D_FF + (jb + 1) * FF_BLOCK))
            act = 0.5 * gate * (1.0 + lax.erf(gate * (2.0 ** -0.5)))
            hid_buf[rows, jb * FF_BLOCK:(jb + 1) * FF_BLOCK] = (act * val).astype(jnp.bfloat16)
        for q0 in range(r0, r0 + sub, sub // 4):
            quarter = slice(q0, q0 + sub // 4)
            y = x_ref[quarter, :] + _dot(hid_buf[quarter, :], w_down_ref[...])
            o_ref[quarter, :] = _rms_norm(y, g_final_ref[...])


def _resident(shape):
    zeros = (0,) * len(shape)
    return pl.BlockSpec(shape, lambda i: zeros, pipeline_mode=pl.Buffered(1))


def _whole(shape):
    zeros = (0,) * len(shape)
    return pl.BlockSpec(shape, lambda i: zeros)


def _rel_vector(rel_bias):
    assert CHUNK - 1 <= MAX_REL <= WIN_KEYS - CHUNK - 1
    n_clipped = WIN_KEYS - CHUNK - MAX_REL
    far = jnp.broadcast_to(rel_bias[:, 2 * MAX_REL:], (ATTN_HEADS, n_clipped))
    near = rel_bias[:, MAX_REL - (CHUNK - 1):2 * MAX_REL][:, ::-1]
    unused = jnp.zeros((ATTN_HEADS, 1), rel_bias.dtype)
    return jnp.concatenate([unused, far, near], axis=1).astype(jnp.float32)


def _block_diag(w):
    g, c, _ = w.shape
    eye = jnp.eye(g, dtype=w.dtype)
    return (eye[:, None, :, None] * w[:, :, None, :]).reshape(g * c, g * c)


def kernel(x, mem, norm_mix_g, norm_mem_g, w_in, b_gate, w_pool, pool_scale, rel_bias, w_mem_kv,
           w_up_pool, w_up_attn, w_up_mem, w_out, norm_ffn_g, w_ffn_up, conv_w, conv_b, w_ffn_down,
           norm_final_g):
    batch, seq, d = x.shape
    assert batch == 1 and d == D_MODEL and seq % SEQ_TILE == 0
    assert norm_mix_g.shape[0] == 1, "single layer"
    bf16 = jnp.bfloat16
    f32 = jnp.float32
    ts = SEQ_TILE
    n_tiles = seq // ts
    mem_len = mem.shape[1]
    params = pltpu.CompilerParams(dimension_semantics=("arbitrary",), vmem_limit_bytes=VMEM_LIMIT)

    mixer_w = (w_in[0], w_up_pool[0], w_up_attn[0], w_up_mem[0], w_out[0], _block_diag(w_pool[0]))
    prep_in = (mem[0], norm_mem_g.astype(f32), w_mem_kv[0])
    assert all(w.shape[0] % (PREP_STEPS * BF16_ROWS) == 0 for w in mixer_w)
    w_slabs = [pl.BlockSpec((w.shape[0] // PREP_STEPS, w.shape[1]), lambda i: (i, 0)) for w in mixer_w]
    prep_out = pl.pallas_call(
        _prep_kernel,
        grid=(PREP_STEPS,),
        in_specs=[_resident(a.shape) for a in prep_in] + w_slabs,
        out_specs=[_whole((mem_len, MEM_WIDTH)), _whole((MEM_HEADS // 2, mem_len, 2 * PAIR))] + w_slabs,
        out_shape=[jax.ShapeDtypeStruct((mem_len, MEM_WIDTH), bf16),
                   jax.ShapeDtypeStruct((MEM_HEADS // 2, mem_len, 2 * PAIR), bf16)]
                  + [jax.ShapeDtypeStruct(w.shape, bf16) for w in mixer_w],
        compiler_params=params,
        name="prep",
    )(*prep_in, *mixer_w)
    km, vm, w_in_bf16, w_up_pool_bf16, w_up_attn_bf16, w_up_mem_bf16, w_out_bf16, w_pool_bf16 = prep_out

    tile_spec = pl.BlockSpec((ts, D_MODEL), lambda i: (i, 0))
    mixer_inputs = (
        norm_mix_g.astype(f32), w_in_bf16, b_gate.astype(f32), w_pool_bf16,
        pool_scale.astype(f32), _rel_vector(rel_bias[0]), km, vm,
        w_up_pool_bf16, w_up_attn_bf16, w_up_mem_bf16, w_out_bf16,
    )
    ffn_w = (w_ffn_up[0], w_ffn_down[0])
    slab_steps = (1, 2)
    slab_specs = [pl.BlockSpec((w.shape[0] * k // n_tiles, w.shape[1]), functools.partial(lambda i, k: (i // k, 0), k=k))
                  for w, k in zip(ffn_w, slab_steps)]
    assert all((w.shape[0] * k) % (n_tiles * BF16_ROWS) == 0 for w, k in zip(ffn_w, slab_steps))
    x1, w_up_bf16, w_down_bf16 = pl.pallas_call(
        _mixer_kernel,
        grid=(n_tiles,),
        in_specs=[pl.BlockSpec(memory_space=pltpu.SMEM), tile_spec] + [_resident(a.shape) for a in mixer_inputs] + slab_specs,
        out_specs=[tile_spec] + slab_specs,
        out_shape=[jax.ShapeDtypeStruct((seq, D_MODEL), f32)] + [jax.ShapeDtypeStruct(w.shape, bf16) for w in ffn_w],
        scratch_shapes=[
            pltpu.VMEM((ts, D_MODEL), bf16),
            pltpu.VMEM((POOL_TAIL + ts, POOL_WIDTH), f32),
            pltpu.VMEM((2, ts, ATTN_WIDTH), bf16),
            pltpu.VMEM((KV_CARRY + ts, ATTN_WIDTH), bf16),
            pltpu.VMEM((KV_CARRY + ts, 2 * ATTN_WIDTH), bf16),
            pltpu.VMEM((2, ts, MEM_WIDTH), bf16),
            pltpu.VMEM((ts, POOL_WIDTH), bf16),
            pltpu.VMEM((ts, ATTN_WIDTH), bf16),
            pltpu.VMEM((ts, MEM_WIDTH), bf16),
            pltpu.VMEM((ts, D_MODEL), bf16),
            pltpu.VMEM((ts, D_MODEL), f32),
            pltpu.VMEM((ts // MEM_ROWS, MEM_HEADS // 2, 2 * MEM_ROWS, mem_len), bf16),
            pltpu.VMEM((ATTN_HEADS // 2, 4 * CHUNK, WIN_KEYS), f32),
            pltpu.VMEM((ATTN_HEADS // 2, 4 * CHUNK, WIN_KEYS), f32),
            pltpu.VMEM((ATTN_HEADS // 2, 4 * CHUNK, WIN_KEYS), f32),
            pltpu.VMEM((ATTN_HEADS // 2, 4 * CHUNK, LANES), f32),
            pltpu.VMEM((ATTN_HEADS // 2, 4 * CHUNK, LANES), f32),
            pltpu.VMEM((ATTN_HEADS // 2, 4 * CHUNK, WIN_KEYS), bf16),
            pltpu.VMEM((ATTN_HEADS // 2, 4 * CHUNK, WIN_KEYS), bf16),
        ],
        compiler_params=params,
        name="mixer",
    )(jnp.zeros((1,), jnp.int32), x[0], *mixer_inputs, *ffn_w)

    ffn_inputs = (
        norm_ffn_g.astype(f32), w_up_bf16, conv_w[0].astype(f32), conv_b.astype(f32),
        w_down_bf16, norm_final_g.reshape(1, -1).astype(f32),
    )
    tf = FFN_TILE
    ffn_tile_spec = pl.BlockSpec((tf, D_MODEL), lambda i: (i, 0))
    out = pl.pallas_call(
        _ffn_kernel,
        grid=(seq // tf,),
        in_specs=[ffn_tile_spec] + [_resident(a.shape) for a in ffn_inputs],
        out_specs=ffn_tile_spec,
        out_shape=jax.ShapeDtypeStruct((seq, D_MODEL), x.dtype),
        scratch_shapes=[
            pltpu.VMEM((tf, D_MODEL), bf16),
            pltpu.VMEM((CONV_PAD + FFN_SUB, FF_BLOCK), f32),
            pltpu.VMEM((CONV_PAD, 2 * D_FF), f32),
            pltpu.VMEM((tf, D_FF), bf16),
        ],
        compiler_params=params,
        name="ffn",
    )(x1, *ffn_inputs)
    return out[None]
```

```python
import functools

import jax
import jax.numpy as jnp
from jax import lax
from jax.experimental import pallas as pl
from jax.experimental.pallas import tpu as pltpu

D_MODEL = 1024
CHUNK = 64
HEAD_DIM = 64
POOL_WINDOWS = (2, 4, 8, 16)
POOL_GROUP_DIM = 64
POOL_WIDTH = len(POOL_WINDOWS) * POOL_GROUP_DIM
ATTN_HEADS = 8
ATTN_WIDTH = ATTN_HEADS * HEAD_DIM
BAND_CHUNKS = 9
BAND_KEYS = BAND_CHUNKS * CHUNK
MAX_REL = 128
MEM_HEADS = 4
MEM_WIDTH = MEM_HEADS * HEAD_DIM
N_BRANCH = 3
D_FF = 2816
CONV_WIDTH = 3
RMS_EPS = 1e-6
NEG_INF = -1e30

O_POOL = 0
O_Q = O_POOL + POOL_WIDTH
O_K = O_Q + ATTN_WIDTH
O_V = O_K + ATTN_WIDTH
O_QM = O_V + ATTN_WIDTH
O_GATE = O_QM + MEM_WIDTH

LANES = 128
SUBLANES = 8
BF16_ROWS = 2 * SUBLANES
PAIR = 2 * HEAD_DIM
POOL_TAIL = 16
WIN_KEYS = BAND_KEYS + CHUNK
KV_CARRY = BAND_KEYS - CHUNK
SEQ_TILE = 512
FFN_TILE = 1024
FFN_SUB = 1024
MEM_ROWS = 256
PREP_STEPS = 8
FF_BLOCK = 256
CONV_PAD = SUBLANES
VMEM_LIMIT = 60 * 1024 * 1024


def _rms_norm(x, g):
    y = x * lax.rsqrt(jnp.mean(x * x, axis=-1, keepdims=True) + RMS_EPS)
    return y * g


def _dot(a, b):
    return jnp.dot(a, b, preferred_element_type=jnp.float32)


def _dot_nt(a, b):
    return lax.dot_general(a, b, (((1,), (1,)), ((), ())), preferred_element_type=jnp.float32)


def _prep_kernel(mem_ref, g_ref, w_kv_ref, w_pool_ref, *refs):
    n_w = (len(refs) - 3) // 2
    k_ref, v_ref, pool_ref = refs[n_w:n_w + 3]
    for src, dst in zip(refs[:n_w], refs[n_w + 3:]):
        dst[...] = src[...].astype(jnp.bfloat16)

    @pl.when(pl.program_id(0) == 0)
    def _first():
        pool_ref[...] = jnp.zeros(pool_ref.shape, jnp.bfloat16)
        for grp in range(w_pool_ref.shape[0]):
            blk = slice(grp * POOL_GROUP_DIM, (grp + 1) * POOL_GROUP_DIM)
            pool_ref[blk, blk] = w_pool_ref[grp].astype(jnp.bfloat16)
        mem_len = mem_ref.shape[0]
        mem_n = _rms_norm(mem_ref[...], g_ref[...]).astype(jnp.bfloat16)
        kv = _dot(mem_n, w_kv_ref[...].astype(jnp.bfloat16))
        k_ref[...] = kv[:, :MEM_WIDTH].astype(jnp.bfloat16)
        for p in range(MEM_HEADS // 2):
            v_ref[p, :, 0:PAIR] = kv[:, MEM_WIDTH + p * PAIR:MEM_WIDTH + (p + 1) * PAIR].astype(jnp.bfloat16)
            v_ref[p, :, PAIR:2 * PAIR] = jnp.ones((mem_len, PAIR), jnp.bfloat16)


def _mixer_kernel(zero_ref, x_ref, g_ref, w_in_ref, b_gate_ref, w_pool_ref, pool_scale_ref, rel_vec_ref,
                  km_ref, vm_ref, w_up_pool_ref, w_up_attn_ref, w_up_mem_ref, w_out_ref,
                  w_ffn_up_ref, w_ffn_down_ref,
                  o_ref, w_ffn_up_bf16_ref, w_ffn_down_bf16_ref,
                  h_buf, u_buf, q_buf, k_buf, v_buf, qm_buf, pool_buf, attn_buf, mem_buf, merged_buf,
                  part_buf, pm_buf, bias_buf, s_buf0, s_buf1, m_buf0, m_buf1, p_buf0, p_buf1):
    s_buf, m_buf, p_buf = (s_buf0, s_buf1), (m_buf0, m_buf1), (p_buf0, p_buf1)
    ts = x_ref.shape[0]
    tile = pl.program_id(0)

    @pl.when(tile == 0)
    def _init():
        u_buf[0:POOL_TAIL, :] = jnp.zeros((POOL_TAIL, POOL_WIDTH), jnp.float32)
        k_buf[0:KV_CARRY, :] = jnp.zeros((KV_CARRY, ATTN_WIDTH), jnp.bfloat16)
        ext_lane = lax.broadcasted_iota(jnp.int32, v_buf.shape, 1) % (2 * PAIR)
        v_buf[...] = jnp.where(ext_lane >= PAIR, 1.0, 0.0).astype(jnp.bfloat16)
        win_lane = lax.broadcasted_iota(jnp.int32, (ATTN_HEADS, WIN_KEYS), 1)
        vecs = rel_vec_ref[...]
        for qi in range(CHUNK):
            first = jnp.where(win_lane >= BAND_KEYS, NEG_INF, pltpu.roll(vecs, qi + BAND_KEYS, axis=1))
            second = jnp.where(win_lane < CHUNK, NEG_INF, pltpu.roll(vecs, qi, axis=1) if qi else vecs)
            for h in range(ATTN_HEADS):
                r = (h % 2) * 2 * CHUNK + qi
                bias_buf[h // 2, r:r + 1, :] = first[h:h + 1, :]
                bias_buf[h // 2, r + CHUNK:r + CHUNK + 1, :] = second[h:h + 1, :]

    w_ffn_up_bf16_ref[...] = w_ffn_up_ref[...].astype(jnp.bfloat16)
    w_ffn_down_bf16_ref[...] = w_ffn_down_ref[...].astype(jnp.bfloat16)

    h_buf[...] = _rms_norm(x_ref[...], g_ref[...]).astype(jnp.bfloat16)

    u_buf[POOL_TAIL:POOL_TAIL + ts, :] = _dot(h_buf[...], w_in_ref[:, O_POOL:O_Q])
    def split_heads(q, even_ref, odd_ref):
        odd = (lax.broadcasted_iota(jnp.int32, q.shape, 1) // HEAD_DIM) % 2 == 1
        q = q * (HEAD_DIM ** -0.5)
        even_ref[...] = jnp.where(odd, 0.0, q).astype(jnp.bfloat16)
        odd_ref[...] = jnp.where(odd, q, 0.0).astype(jnp.bfloat16)

    split_heads(_dot(h_buf[...], w_in_ref[:, O_Q:O_K]), q_buf.at[0], q_buf.at[1])
    k_buf[KV_CARRY:KV_CARRY + ts, :] = _dot(h_buf[...], w_in_ref[:, O_K:O_V]).astype(jnp.bfloat16)
    v = _dot(h_buf[...], w_in_ref[:, O_V:O_QM]).astype(jnp.bfloat16)
    for p in range(ATTN_HEADS // 2):
        v_buf[KV_CARRY:KV_CARRY + ts, p * 2 * PAIR:p * 2 * PAIR + PAIR] = v[:, p * PAIR:(p + 1) * PAIR]
    split_heads(_dot(h_buf[...], w_in_ref[:, O_QM:O_GATE]), qm_buf.at[0], qm_buf.at[1])

    u = u_buf[POOL_TAIL:POOL_TAIL + ts, :]
    lane_group = lax.broadcasted_iota(jnp.int32, (ts, POOL_WIDTH), 1) // POOL_GROUP_DIM
    run = u
    win_sum = jnp.zeros_like(u)
    for j in range(1, POOL_TAIL + 1):
        if j in POOL_WINDOWS:
            win_sum = jnp.where(lane_group == POOL_WINDOWS.index(j), run, win_sum)
        if j < POOL_TAIL:
            run = run + u_buf[POOL_TAIL - j:POOL_TAIL - j + ts, :]
    window = jnp.left_shift(2, lane_group)
    pos1 = tile * ts + lax.broadcasted_iota(jnp.int32, (ts, POOL_WIDTH), 0) + 1
    cnt = jnp.minimum(pos1, window).astype(jnp.float32)
    pooled = (win_sum / cnt - u).astype(jnp.bfloat16)
    pool_buf[...] = (_dot(pooled, w_pool_ref[...]) * pool_scale_ref[...]).astype(jnp.bfloat16)
    u_buf[0:POOL_TAIL, :] = u_buf[ts:ts + POOL_TAIL, :]

    def pair_cols(h):
        return slice((h // 2) * PAIR, (h // 2 + 1) * PAIR)

    staged = zero_ref[0]

    def pair_output(acc):
        half = acc.shape[0] // 2
        out = acc[:, :PAIR] / acc[:, PAIR:]
        lane_is_odd = lax.broadcasted_iota(jnp.int32, (half, PAIR), 1) >= HEAD_DIM
        return jnp.where(lane_is_odd, out[half:], out[:half]).astype(jnp.bfloat16)

    mem_len = km_ref.shape[0]
    n_mem = ts // MEM_ROWS

    def mem_score(r):
        r0 = r * MEM_ROWS
        for h in range(MEM_HEADS):
            s = _dot_nt(qm_buf[h % 2, r0:r0 + MEM_ROWS, pair_cols(h)], km_ref[:, pair_cols(h)])
            e = jnp.exp(s - jnp.max(s, axis=-1, keepdims=True))
            pm_buf[r, h // 2, (h % 2) * MEM_ROWS:(h % 2 + 1) * MEM_ROWS, :] = e.astype(jnp.bfloat16)

    def mem_value(r):
        r0 = r * MEM_ROWS
        for p in range(MEM_HEADS // 2):
            acc = _dot(pm_buf[staged + r, p], vm_ref[p])
            mem_buf[r0:r0 + MEM_ROWS, pair_cols(2 * p)] = pair_output(acc)

    merge_blk = 2 * LANES
    n_merge = D_MODEL // merge_blk

    def merge_cols(nb):
        return slice(nb * merge_blk, (nb + 1) * merge_blk)

    def gated(b, branch_buf, w_up_ref, nb):
        lo = b * D_MODEL + nb * merge_blk
        logits = _dot(h_buf[...], w_in_ref[:, O_GATE + lo:O_GATE + lo + merge_blk]) + b_gate_ref[:, lo:lo + merge_blk]
        return jax.nn.sigmoid(logits) * _dot(branch_buf[...], w_up_ref[:, merge_cols(nb)])

    win_col = lax.broadcasted_iota(jnp.int32, (1, WIN_KEYS), 1)
    pad_rows = jnp.where(tile == 0, KV_CARRY, 0)

    def ext_cols(p):
        return slice(p * 2 * PAIR, (p + 1) * 2 * PAIR)

    n_k = WIN_KEYS // LANES
    n_cp = ts // (2 * CHUNK)

    def score_phase(cp):
        base = cp * 2 * CHUNK
        slot = cp % 2
        pen = jnp.where(win_col + base < pad_rows, NEG_INF, 0.0)
        for p in range(ATTN_HEADS // 2):
            q_rows = jnp.concatenate([q_buf[0, base:base + 2 * CHUNK, pair_cols(2 * p)],
                                      q_buf[1, base:base + 2 * CHUNK, pair_cols(2 * p)]], axis=0)
            s = _dot_nt(q_rows, k_buf[base:base + WIN_KEYS, pair_cols(2 * p)]) + bias_buf[p] + pen
            s_buf[slot][p] = s
            part = s[:, 0:LANES]
            for k in range(1, n_k):
                part = jnp.maximum(part, s[:, k * LANES:(k + 1) * LANES])
            m_buf[slot][p] = jnp.broadcast_to(jnp.max(part, axis=-1, keepdims=True), (4 * CHUNK, LANES))

    def softmax_phase(cp):
        slot = cp % 2
        for p in range(ATTN_HEADS // 2):
            m = m_buf[slot][staged + p]
            for k in range(n_k):
                cols = slice(k * LANES, (k + 1) * LANES)
                p_buf[slot][p, :, cols] = jnp.exp(s_buf[slot][staged + p, :, cols] - m).astype(jnp.bfloat16)

    def value_phase(cp):
        base = cp * 2 * CHUNK
        slot = cp % 2
        for p in range(ATTN_HEADS // 2):
            acc = _dot(p_buf[slot][staged + p], v_buf[base:base + WIN_KEYS, ext_cols(p)])
            attn_buf[base:base + 2 * CHUNK, pair_cols(2 * p)] = pair_output(acc)

    mem_steps = 2
    mem_per_step = n_mem // mem_steps
    assert mem_steps + n_merge <= n_cp + 2
    for step in range(n_cp + 2):
        mem_blocks = range(step * mem_per_step, (step + 1) * mem_per_step) if step < mem_steps else ()
        for r in mem_blocks:
            mem_score(r)
        if step < n_cp:
            score_phase(step)
        if 0 <= step - mem_steps < n_merge:
            nb = step - mem_steps
            part_buf[:, merge_cols(nb)] = (gated(0, pool_buf, w_up_pool_ref, nb)
                                           + gated(2, mem_buf, w_up_mem_ref, nb))
        if 0 <= step - 1 < n_cp:
            softmax_phase(step - 1)
        if 0 <= step - 2 < n_cp:
            value_phase(step - 2)
        for r in mem_blocks:
            mem_value(r)

    k_buf[0:KV_CARRY, :] = k_buf[ts:ts + KV_CARRY, :]
    for p in range(ATTN_HEADS // 2):
        vcols = slice(p * 2 * PAIR, p * 2 * PAIR + PAIR)
        v_buf[0:KV_CARRY, vcols] = v_buf[ts:ts + KV_CARRY, vcols]

    for nb in range(n_merge):
        merged = part_buf[:, merge_cols(nb)] + gated(1, attn_buf, w_up_attn_ref, nb)
        merged_buf[:, merge_cols(nb)] = merged.astype(jnp.bfloat16)
    o_ref[...] = x_ref[...] + _dot(merged_buf[...], w_out_ref[...])


def _ffn_kernel(x_ref, g_ref, w_up_ref, conv_w_ref, conv_b_ref, w_down_ref, g_final_ref, o_ref,
                h_buf, a_buf, tail_buf, hid_buf):
    sub = a_buf.shape[0] - CONV_PAD

    @pl.when(pl.program_id(0) == 0)
    def _init():
        tail_buf[...] = jnp.zeros_like(tail_buf)

    def conv_up(rows, cols):
        a = _dot(h_buf[rows, :], w_up_ref[:, cols])
        a_buf[0:CONV_PAD, :] = tail_buf[:, cols]
        a_buf[CONV_PAD:CONV_PAD + sub, :] = a
        tail_buf[:, cols] = a_buf[sub:sub + CONV_PAD, :]
        out = conv_b_ref[:, cols] + a * conv_w_ref[CONV_WIDTH - 1:CONV_WIDTH, cols]
        for t in range(CONV_WIDTH - 1):
            back = CONV_WIDTH - 1 - t
            out = out + a_buf[CONV_PAD - back:CONV_PAD - back + sub, :] * conv_w_ref[t:t + 1, cols]
        return out

    for r0 in range(0, x_ref.shape[0], sub):
        rows = slice(r0, r0 + sub)
        h_buf[rows, :] = _rms_norm(x_ref[rows, :], g_ref[...]).astype(jnp.bfloat16)
        for jb in range(D_FF // FF_BLOCK):
            gate = conv_up(rows, slice(jb * FF_BLOCK, (jb + 1) * FF_BLOCK))
            val = conv_up(rows, slice(D_FF + jb * FF_BLOCK, D_FF + (jb + 1) * FF_BLOCK))
            act = 0.5 * gate * (1.0 + lax.erf(gate * (2.0 ** -0.5)))
            hid_buf[rows, jb * FF_BLOCK:(jb + 1) * FF_BLOCK] = (act * val).astype(jnp.bfloat16)
        for q0 in range(r0, r0 + sub, sub // 4):
            quarter = slice(q0, q0 + sub // 4)
            y = x_ref[quarter, :] + _dot(hid_buf[quarter, :], w_down_ref[...])
            o_ref[quarter, :] = _rms_norm(y, g_final_ref[...])


def _resident(shape):
    zeros = (0,) * len(shape)
    return pl.BlockSpec(shape, lambda i: zeros, pipeline_mode=pl.Buffered(1))


def _whole(shape):
    zeros = (0,) * len(shape)
    return pl.BlockSpec(shape, lambda i: zeros)


def _rel_vector(rel_bias):
    assert CHUNK - 1 <= MAX_REL <= WIN_KEYS - CHUNK - 1
    n_clipped = WIN_KEYS - CHUNK - MAX_REL
    far = jnp.broadcast_to(rel_bias[:, 2 * MAX_REL:], (ATTN_HEADS, n_clipped))
    near = rel_bias[:, MAX_REL - (CHUNK - 1):2 * MAX_REL][:, ::-1]
    unused = jnp.zeros((ATTN_HEADS, 1), rel_bias.dtype)
    return jnp.concatenate([unused, far, near], axis=1).astype(jnp.float32)


def kernel(x, mem, norm_mix_g, norm_mem_g, w_in, b_gate, w_pool, pool_scale, rel_bias, w_mem_kv,
           w_up_pool, w_up_attn, w_up_mem, w_out, norm_ffn_g, w_ffn_up, conv_w, conv_b, w_ffn_down,
           norm_final_g):
    batch, seq, d = x.shape
    assert batch == 1 and d == D_MODEL and seq % SEQ_TILE == 0
    assert norm_mix_g.shape[0] == 1, "single layer"
    bf16 = jnp.bfloat16
    f32 = jnp.float32
    ts = SEQ_TILE
    n_tiles = seq // ts
    mem_len = mem.shape[1]
    params = pltpu.CompilerParams(dimension_semantics=("arbitrary",), vmem_limit_bytes=VMEM_LIMIT)

    mixer_w = (w_in[0], w_up_pool[0], w_up_attn[0], w_up_mem[0], w_out[0])
    prep_in = (mem[0], norm_mem_g.astype(f32), w_mem_kv[0], w_pool[0])
    assert all(w.shape[0] % (PREP_STEPS * BF16_ROWS) == 0 for w in mixer_w)
    w_slabs = [pl.BlockSpec((w.shape[0] // PREP_STEPS, w.shape[1]), lambda i: (i, 0)) for w in mixer_w]
    prep_out = pl.pallas_call(
        _prep_kernel,
        grid=(PREP_STEPS,),
        in_specs=[_resident(a.shape) for a in prep_in] + w_slabs,
        out_specs=[_whole((mem_len, MEM_WIDTH)), _whole((MEM_HEADS // 2, mem_len, 2 * PAIR)),
                   _whole((POOL_WIDTH, POOL_WIDTH))] + w_slabs,
        out_shape=[jax.ShapeDtypeStruct((mem_len, MEM_WIDTH), bf16),
                   jax.ShapeDtypeStruct((MEM_HEADS // 2, mem_len, 2 * PAIR), bf16),
                   jax.ShapeDtypeStruct((POOL_WIDTH, POOL_WIDTH), bf16)]
                  + [jax.ShapeDtypeStruct(w.shape, bf16) for w in mixer_w],
        compiler_params=params,
        name="prep",
    )(*prep_in, *mixer_w)
    km, vm, w_pool_bf16, w_in_bf16, w_up_pool_bf16, w_up_attn_bf16, w_up_mem_bf16, w_out_bf16 = prep_out

    tile_spec = pl.BlockSpec((ts, D_MODEL), lambda i: (i, 0))
    mixer_inputs = (
        norm_mix_g.astype(f32), w_in_bf16, b_gate.astype(f32), w_pool_bf16,
        pool_scale.astype(f32), _rel_vector(rel_bias[0]), km, vm,
        w_up_pool_bf16, w_up_attn_bf16, w_up_mem_bf16, w_out_bf16,
    )
    ffn_w = (w_ffn_up[0], w_ffn_down[0])
    slab_steps = (1, 2)
    slab_specs = [pl.BlockSpec((w.shape[0] * k // n_tiles, w.shape[1]), functools.partial(lambda i, k: (i // k, 0), k=k))
                  for w, k in zip(ffn_w, slab_steps)]
    assert all((w.shape[0] * k) % (n_tiles * BF16_ROWS) == 0 for w, k in zip(ffn_w, slab_steps))
    x1, w_up_bf16, w_down_bf16 = pl.pallas_call(
        _mixer_kernel,
        grid=(n_tiles,),
        in_specs=[pl.BlockSpec(memory_space=pltpu.SMEM), tile_spec] + [_resident(a.shape) for a in mixer_inputs] + slab_specs,
        out_specs=[tile_spec] + slab_specs,
        out_shape=[jax.ShapeDtypeStruct((seq, D_MODEL), f32)] + [jax.ShapeDtypeStruct(w.shape, bf16) for w in ffn_w],
        scratch_shapes=[
            pltpu.VMEM((ts, D_MODEL), bf16),
            pltpu.VMEM((POOL_TAIL + ts, POOL_WIDTH), f32),
            pltpu.VMEM((2, ts, ATTN_WIDTH), bf16),
            pltpu.VMEM((KV_CARRY + ts, ATTN_WIDTH), bf16),
            pltpu.VMEM((KV_CARRY + ts, 2 * ATTN_WIDTH), bf16),
            pltpu.VMEM((2, ts, MEM_WIDTH), bf16),
            pltpu.VMEM((ts, POOL_WIDTH), bf16),
            pltpu.VMEM((ts, ATTN_WIDTH), bf16),
            pltpu.VMEM((ts, MEM_WIDTH), bf16),
            pltpu.VMEM((ts, D_MODEL), bf16),
            pltpu.VMEM((ts, D_MODEL), f32),
            pltpu.VMEM((ts // MEM_ROWS, MEM_HEADS // 2, 2 * MEM_ROWS, mem_len), bf16),
            pltpu.VMEM((ATTN_HEADS // 2, 4 * CHUNK, WIN_KEYS), f32),
            pltpu.VMEM((ATTN_HEADS // 2, 4 * CHUNK, WIN_KEYS), f32),
            pltpu.VMEM((ATTN_HEADS // 2, 4 * CHUNK, WIN_KEYS), f32),
            pltpu.VMEM((ATTN_HEADS // 2, 4 * CHUNK, LANES), f32),
            pltpu.VMEM((ATTN_HEADS // 2, 4 * CHUNK, LANES), f32),
            pltpu.VMEM((ATTN_HEADS // 2, 4 * CHUNK, WIN_KEYS), bf16),
            pltpu.VMEM((ATTN_HEADS // 2, 4 * CHUNK, WIN_KEYS), bf16),
        ],
        compiler_params=params,
        name="mixer",
    )(jnp.zeros((1,), jnp.int32), x[0], *mixer_inputs, *ffn_w)

    ffn_inputs = (
        norm_ffn_g.astype(f32), w_up_bf16, conv_w[0].astype(f32), conv_b.astype(f32),
        w_down_bf16, norm_final_g.reshape(1, -1).astype(f32),
    )
    tf = FFN_TILE
    ffn_tile_spec = pl.BlockSpec((tf, D_MODEL), lambda i: (i, 0))
    out = pl.pallas_call(
        _ffn_kernel,
        grid=(seq // tf,),
        in_specs=[ffn_tile_spec] + [_resident(a.shape) for a in ffn_inputs],
        out_specs=ffn_tile_spec,
        out_shape=jax.ShapeDtypeStruct((seq, D_MODEL), x.dtype),
        scratch_shapes=[
            pltpu.VMEM((tf, D_MODEL), bf16),
            pltpu.VMEM((CONV_PAD + FFN_SUB, FF_BLOCK), f32),
            pltpu.VMEM((CONV_PAD, 2 * D_FF), f32),
            pltpu.VMEM((tf, D_FF), bf16),
        ],
        compiler_params=params,
        name="ffn",
    )(x1, *ffn_inputs)
    return out[None]
```

```python
import functools

import jax
import jax.numpy as jnp
from jax import lax
from jax.experimental import pallas as pl
from jax.experimental.pallas import tpu as pltpu

D_MODEL = 1024
CHUNK = 64
HEAD_DIM = 64
POOL_WINDOWS = (2, 4, 8, 16)
POOL_GROUP_DIM = 64
POOL_WIDTH = len(POOL_WINDOWS) * POOL_GROUP_DIM
ATTN_HEADS = 8
ATTN_WIDTH = ATTN_HEADS * HEAD_DIM
BAND_CHUNKS = 9
BAND_KEYS = BAND_CHUNKS * CHUNK
MAX_REL = 128
MEM_HEADS = 4
MEM_WIDTH = MEM_HEADS * HEAD_DIM
N_BRANCH = 3
D_FF = 2816
CONV_WIDTH = 3
RMS_EPS = 1e-6
NEG_INF = -1e30

O_POOL = 0
O_Q = O_POOL + POOL_WIDTH
O_K = O_Q + ATTN_WIDTH
O_V = O_K + ATTN_WIDTH
O_QM = O_V + ATTN_WIDTH
O_GATE = O_QM + MEM_WIDTH

LANES = 128
SUBLANES = 8
BF16_ROWS = 2 * SUBLANES
PAIR = 2 * HEAD_DIM
POOL_TAIL = 16
WIN_KEYS = BAND_KEYS + CHUNK
KV_CARRY = BAND_KEYS - CHUNK
SEQ_TILE = 512
FFN_TILE = 1024
FFN_SUB = 1024
MEM_ROWS = 512
PREP_STEPS = 8
FF_BLOCK = 256
CONV_PAD = SUBLANES
VMEM_LIMIT = 60 * 1024 * 1024


def _rms_norm(x, g):
    y = x * lax.rsqrt(jnp.mean(x * x, axis=-1, keepdims=True) + RMS_EPS)
    return y * g


def _dot(a, b):
    return jnp.dot(a, b, preferred_element_type=jnp.float32)


def _dot_nt(a, b):
    return lax.dot_general(a, b, (((1,), (1,)), ((), ())), preferred_element_type=jnp.float32)


def _prep_kernel(mem_ref, g_ref, w_kv_ref, *refs):
    n_w = (len(refs) - 2) // 2
    k_ref, v_ref = refs[n_w:n_w + 2]
    for src, dst in zip(refs[:n_w], refs[n_w + 2:]):
        dst[...] = src[...].astype(jnp.bfloat16)

    @pl.when(pl.program_id(0) == 0)
    def _first():
        mem_len = mem_ref.shape[0]
        mem_n = _rms_norm(mem_ref[...], g_ref[...]).astype(jnp.bfloat16)
        kv = _dot(mem_n, w_kv_ref[...].astype(jnp.bfloat16))
        k_ref[...] = kv[:, :MEM_WIDTH].astype(jnp.bfloat16)
        for p in range(MEM_HEADS // 2):
            v_ref[p, :, 0:PAIR] = kv[:, MEM_WIDTH + p * PAIR:MEM_WIDTH + (p + 1) * PAIR].astype(jnp.bfloat16)
            v_ref[p, :, PAIR:2 * PAIR] = jnp.ones((mem_len, PAIR), jnp.bfloat16)


def _mixer_kernel(zero_ref, x_ref, g_ref, w_in_ref, b_gate_ref, w_pool_ref, pool_scale_ref, rel_vec_ref,
                  km_ref, vm_ref, w_up_pool_ref, w_up_attn_ref, w_up_mem_ref, w_out_ref,
                  w_ffn_up_ref, w_ffn_down_ref,
                  o_ref, w_ffn_up_bf16_ref, w_ffn_down_bf16_ref,
                  h_buf, u_buf, q_buf, k_buf, v_buf, qm_buf, pool_buf, attn_buf, mem_buf, merged_buf,
                  part_buf, pm_buf, bias_buf, s_buf0, s_buf1, m_buf0, m_buf1, p_buf0, p_buf1):
    s_buf, m_buf, p_buf = (s_buf0, s_buf1), (m_buf0, m_buf1), (p_buf0, p_buf1)
    ts = x_ref.shape[0]
    tile = pl.program_id(0)

    @pl.when(tile == 0)
    def _init():
        u_buf[0:POOL_TAIL, :] = jnp.zeros((POOL_TAIL, POOL_WIDTH), jnp.float32)
        k_buf[0:KV_CARRY, :] = jnp.zeros((KV_CARRY, ATTN_WIDTH), jnp.bfloat16)
        ext_lane = lax.broadcasted_iota(jnp.int32, v_buf.shape, 1) % (2 * PAIR)
        v_buf[...] = jnp.where(ext_lane >= PAIR, 1.0, 0.0).astype(jnp.bfloat16)
        win_lane = lax.broadcasted_iota(jnp.int32, (ATTN_HEADS, WIN_KEYS), 1)
        vecs = rel_vec_ref[...]
        for qi in range(CHUNK):
            first = jnp.where(win_lane >= BAND_KEYS, NEG_INF, pltpu.roll(vecs, qi + BAND_KEYS, axis=1))
            second = jnp.where(win_lane < CHUNK, NEG_INF, pltpu.roll(vecs, qi, axis=1) if qi else vecs)
            for h in range(ATTN_HEADS):
                r = (h % 2) * 2 * CHUNK + qi
                bias_buf[h // 2, r:r + 1, :] = first[h:h + 1, :]
                bias_buf[h // 2, r + CHUNK:r + CHUNK + 1, :] = second[h:h + 1, :]

    w_ffn_up_bf16_ref[...] = w_ffn_up_ref[...].astype(jnp.bfloat16)
    w_ffn_down_bf16_ref[...] = w_ffn_down_ref[...].astype(jnp.bfloat16)

    h_buf[...] = _rms_norm(x_ref[...], g_ref[...]).astype(jnp.bfloat16)

    u_buf[POOL_TAIL:POOL_TAIL + ts, :] = _dot(h_buf[...], w_in_ref[:, O_POOL:O_Q])
    def split_heads(q, even_ref, odd_ref):
        odd = (lax.broadcasted_iota(jnp.int32, q.shape, 1) // HEAD_DIM) % 2 == 1
        q = q * (HEAD_DIM ** -0.5)
        even_ref[...] = jnp.where(odd, 0.0, q).astype(jnp.bfloat16)
        odd_ref[...] = jnp.where(odd, q, 0.0).astype(jnp.bfloat16)

    split_heads(_dot(h_buf[...], w_in_ref[:, O_Q:O_K]), q_buf.at[0], q_buf.at[1])
    k_buf[KV_CARRY:KV_CARRY + ts, :] = _dot(h_buf[...], w_in_ref[:, O_K:O_V]).astype(jnp.bfloat16)
    v = _dot(h_buf[...], w_in_ref[:, O_V:O_QM]).astype(jnp.bfloat16)
    for p in range(ATTN_HEADS // 2):
        v_buf[KV_CARRY:KV_CARRY + ts, p * 2 * PAIR:p * 2 * PAIR + PAIR] = v[:, p * PAIR:(p + 1) * PAIR]
    split_heads(_dot(h_buf[...], w_in_ref[:, O_QM:O_GATE]), qm_buf.at[0], qm_buf.at[1])

    u = u_buf[POOL_TAIL:POOL_TAIL + ts, :]
    lane_group = lax.broadcasted_iota(jnp.int32, (ts, POOL_WIDTH), 1) // POOL_GROUP_DIM
    run = u
    win_sum = jnp.zeros_like(u)
    for j in range(1, POOL_TAIL + 1):
        if j in POOL_WINDOWS:
            win_sum = jnp.where(lane_group == POOL_WINDOWS.index(j), run, win_sum)
        if j < POOL_TAIL:
            run = run + u_buf[POOL_TAIL - j:POOL_TAIL - j + ts, :]
    window = jnp.left_shift(2, lane_group)
    pos1 = tile * ts + lax.broadcasted_iota(jnp.int32, (ts, POOL_WIDTH), 0) + 1
    cnt = jnp.minimum(pos1, window).astype(jnp.float32)
    pooled = (win_sum / cnt - u).astype(jnp.bfloat16)
    pool_buf[...] = (_dot(pooled, w_pool_ref[...]) * pool_scale_ref[...]).astype(jnp.bfloat16)
    u_buf[0:POOL_TAIL, :] = u_buf[ts:ts + POOL_TAIL, :]

    def pair_cols(h):
        return slice((h // 2) * PAIR, (h // 2 + 1) * PAIR)

    staged = zero_ref[0]

    def pair_output(acc):
        half = acc.shape[0] // 2
        out = acc[:, :PAIR] / acc[:, PAIR:]
        lane_is_odd = lax.broadcasted_iota(jnp.int32, (half, PAIR), 1) >= HEAD_DIM
        return jnp.where(lane_is_odd, out[half:], out[:half]).astype(jnp.bfloat16)

    mem_len = km_ref.shape[0]
    n_mem = ts // MEM_ROWS

    def mem_score(r):
        r0 = r * MEM_ROWS
        for h in range(MEM_HEADS):
            s = _dot_nt(qm_buf[h % 2, r0:r0 + MEM_ROWS, pair_cols(h)], km_ref[:, pair_cols(h)])
            e = jnp.exp(s - jnp.max(s, axis=-1, keepdims=True))
            pm_buf[r, h // 2, (h % 2) * MEM_ROWS:(h % 2 + 1) * MEM_ROWS, :] = e.astype(jnp.bfloat16)

    def mem_value(r):
        r0 = r * MEM_ROWS
        for p in range(MEM_HEADS // 2):
            acc = _dot(pm_buf[staged + r, p], vm_ref[p])
            mem_buf[r0:r0 + MEM_ROWS, pair_cols(2 * p)] = pair_output(acc)

    merge_blk = 2 * LANES
    n_merge = D_MODEL // merge_blk

    def merge_cols(nb):
        return slice(nb * merge_blk, (nb + 1) * merge_blk)

    def gated(b, branch_buf, w_up_ref, nb):
        lo = b * D_MODEL + nb * merge_blk
        logits = _dot(h_buf[...], w_in_ref[:, O_GATE + lo:O_GATE + lo + merge_blk]) + b_gate_ref[:, lo:lo + merge_blk]
        return jax.nn.sigmoid(logits) * _dot(branch_buf[...], w_up_ref[:, merge_cols(nb)])

    win_col = lax.broadcasted_iota(jnp.int32, (1, WIN_KEYS), 1)
    pad_rows = jnp.where(tile == 0, KV_CARRY, 0)

    def ext_cols(p):
        return slice(p * 2 * PAIR, (p + 1) * 2 * PAIR)

    n_k = WIN_KEYS // LANES
    n_cp = ts // (2 * CHUNK)

    def score_phase(cp):
        base = cp * 2 * CHUNK
        slot = cp % 2
        pen = jnp.where(win_col + base < pad_rows, NEG_INF, 0.0)
        for p in range(ATTN_HEADS // 2):
            q_rows = jnp.concatenate([q_buf[0, base:base + 2 * CHUNK, pair_cols(2 * p)],
                                      q_buf[1, base:base + 2 * CHUNK, pair_cols(2 * p)]], axis=0)
            s = _dot_nt(q_rows, k_buf[base:base + WIN_KEYS, pair_cols(2 * p)]) + bias_buf[p] + pen
            s_buf[slot][p] = s
            part = s[:, 0:LANES]
            for k in range(1, n_k):
                part = jnp.maximum(part, s[:, k * LANES:(k + 1) * LANES])
            m_buf[slot][p] = jnp.broadcast_to(jnp.max(part, axis=-1, keepdims=True), (4 * CHUNK, LANES))

    def softmax_phase(cp):
        slot = cp % 2
        for p in range(ATTN_HEADS // 2):
            m = m_buf[slot][staged + p]
            for k in range(n_k):
                cols = slice(k * LANES, (k + 1) * LANES)
                p_buf[slot][p, :, cols] = jnp.exp(s_buf[slot][staged + p, :, cols] - m).astype(jnp.bfloat16)

    def value_phase(cp):
        base = cp * 2 * CHUNK
        slot = cp % 2
        for p in range(ATTN_HEADS // 2):
            acc = _dot(p_buf[slot][staged + p], v_buf[base:base + WIN_KEYS, ext_cols(p)])
            attn_buf[base:base + 2 * CHUNK, pair_cols(2 * p)] = pair_output(acc)

    mem_steps = min(2, n_mem)
    mem_per_step = n_mem // mem_steps
    assert mem_steps + n_merge <= n_cp + 2
    for step in range(n_cp + 2):
        mem_blocks = range(step * mem_per_step, (step + 1) * mem_per_step) if step < mem_steps else ()
        for r in mem_blocks:
            mem_score(r)
        if step < n_cp:
            score_phase(step)
        if 0 <= step - mem_steps < n_merge:
            nb = step - mem_steps
            part_buf[:, merge_cols(nb)] = (gated(0, pool_buf, w_up_pool_ref, nb)
                                           + gated(2, mem_buf, w_up_mem_ref, nb))
        if 0 <= step - 1 < n_cp:
            softmax_phase(step - 1)
        if 0 <= step - 2 < n_cp:
            value_phase(step - 2)
        for r in mem_blocks:
            mem_value(r)

    k_buf[0:KV_CARRY, :] = k_buf[ts:ts + KV_CARRY, :]
    for p in range(ATTN_HEADS // 2):
        vcols = slice(p * 2 * PAIR, p * 2 * PAIR + PAIR)
        v_buf[0:KV_CARRY, vcols] = v_buf[ts:ts + KV_CARRY, vcols]

    for nb in range(n_merge):
        merged = part_buf[:, merge_cols(nb)] + gated(1, attn_buf, w_up_attn_ref, nb)
        merged_buf[:, merge_cols(nb)] = merged.astype(jnp.bfloat16)
    o_ref[...] = x_ref[...] + _dot(merged_buf[...], w_out_ref[...])


def _ffn_kernel(x_ref, g_ref, w_up_ref, conv_w_ref, conv_b_ref, w_down_ref, g_final_ref, o_ref,
                h_buf, a_buf, tail_buf, hid_buf):
    sub = a_buf.shape[0] - CONV_PAD

    @pl.when(pl.program_id(0) == 0)
    def _init():
        tail_buf[...] = jnp.zeros_like(tail_buf)

    def conv_up(rows, cols):
        a = _dot(h_buf[rows, :], w_up_ref[:, cols])
        a_buf[0:CONV_PAD, :] = tail_buf[:, cols]
        a_buf[CONV_PAD:CONV_PAD + sub, :] = a
        tail_buf[:, cols] = a_buf[sub:sub + CONV_PAD, :]
        out = conv_b_ref[:, cols] + a * conv_w_ref[CONV_WIDTH - 1:CONV_WIDTH, cols]
        for t in range(CONV_WIDTH - 1):
            back = CONV_WIDTH - 1 - t
            out = out + a_buf[CONV_PAD - back:CONV_PAD - back + sub, :] * conv_w_ref[t:t + 1, cols]
        return out

    for r0 in range(0, x_ref.shape[0], sub):
        rows = slice(r0, r0 + sub)
        h_buf[rows, :] = _rms_norm(x_ref[rows, :], g_ref[...]).astype(jnp.bfloat16)
        for jb in range(D_FF // FF_BLOCK):
            gate = conv_up(rows, slice(jb * FF_BLOCK, (jb + 1) * FF_BLOCK))
            val = conv_up(rows, slice(D_FF + jb * FF_BLOCK, D_FF + (jb + 1) * FF_BLOCK))
            act = 0.5 * gate * (1.0 + lax.erf(gate * (2.0 ** -0.5)))
            hid_buf[rows, jb * FF_BLOCK:(jb + 1) * FF_BLOCK] = (act * val).astype(jnp.bfloat16)
        for q0 in range(r0, r0 + sub, sub // 4):
            quarter = slice(q0, q0 + sub // 4)
            y = x_ref[quarter, :] + _dot(hid_buf[quarter, :], w_down_ref[...])
            o_ref[quarter, :] = _rms_norm(y, g_final_ref[...])


def _resident(shape):
    zeros = (0,) * len(shape)
    return pl.BlockSpec(shape, lambda i: zeros, pipeline_mode=pl.Buffered(1))


def _whole(shape):
    zeros = (0,) * len(shape)
    return pl.BlockSpec(shape, lambda i: zeros)


def _rel_vector(rel_bias):
    assert CHUNK - 1 <= MAX_REL <= WIN_KEYS - CHUNK - 1
    n_clipped = WIN_KEYS - CHUNK - MAX_REL
    far = jnp.broadcast_to(rel_bias[:, 2 * MAX_REL:], (ATTN_HEADS, n_clipped))
    near = rel_bias[:, MAX_REL - (CHUNK - 1):2 * MAX_REL][:, ::-1]
    unused = jnp.zeros((ATTN_HEADS, 1), rel_bias.dtype)
    return jnp.concatenate([unused, far, near], axis=1).astype(jnp.float32)


def _block_diag(w):
    g, c, _ = w.shape
    eye = jnp.eye(g, dtype=w.dtype)
    return (eye[:, None, :, None] * w[:, :, None, :]).reshape(g * c, g * c)


def kernel(x, mem, norm_mix_g, norm_mem_g, w_in, b_gate, w_pool, pool_scale, rel_bias, w_mem_kv,
           w_up_pool, w_up_attn, w_up_mem, w_out, norm_ffn_g, w_ffn_up, conv_w, conv_b, w_ffn_down,
           norm_final_g):
    batch, seq, d = x.shape
    assert batch == 1 and d == D_MODEL and seq % SEQ_TILE == 0
    assert norm_mix_g.shape[0] == 1, "single layer"
    bf16 = jnp.bfloat16
    f32 = jnp.float32
    ts = SEQ_TILE
    n_tiles = seq // ts
    mem_len = mem.shape[1]
    params = pltpu.CompilerParams(dimension_semantics=("arbitrary",), vmem_limit_bytes=VMEM_LIMIT)

    mixer_w = (w_in[0], w_up_pool[0], w_up_attn[0], w_up_mem[0], w_out[0], _block_diag(w_pool[0]))
    prep_in = (mem[0], norm_mem_g.astype(f32), w_mem_kv[0])
    assert all(w.shape[0] % (PREP_STEPS * BF16_ROWS) == 0 for w in mixer_w)
    w_slabs = [pl.BlockSpec((w.shape[0] // PREP_STEPS, w.shape[1]), lambda i: (i, 0)) for w in mixer_w]
    prep_out = pl.pallas_call(
        _prep_kernel,
        grid=(PREP_STEPS,),
        in_specs=[_resident(a.shape) for a in prep_in] + w_slabs,
        out_specs=[_whole((mem_len, MEM_WIDTH)), _whole((MEM_HEADS // 2, mem_len, 2 * PAIR))] + w_slabs,
        out_shape=[jax.ShapeDtypeStruct((mem_len, MEM_WIDTH), bf16),
                   jax.ShapeDtypeStruct((MEM_HEADS // 2, mem_len, 2 * PAIR), bf16)]
                  + [jax.ShapeDtypeStruct(w.shape, bf16) for w in mixer_w],
        compiler_params=params,
        name="prep",
    )(*prep_in, *mixer_w)
    km, vm, w_in_bf16, w_up_pool_bf16, w_up_attn_bf16, w_up_mem_bf16, w_out_bf16, w_pool_bf16 = prep_out

    tile_spec = pl.BlockSpec((ts, D_MODEL), lambda i: (i, 0))
    mixer_inputs = (
        norm_mix_g.astype(f32), w_in_bf16, b_gate.astype(f32), w_pool_bf16,
        pool_scale.astype(f32), _rel_vector(rel_bias[0]), km, vm,
        w_up_pool_bf16, w_up_attn_bf16, w_up_mem_bf16, w_out_bf16,
    )
    ffn_w = (w_ffn_up[0], w_ffn_down[0])
    slab_steps = (1, 2)
    slab_specs = [pl.BlockSpec((w.shape[0] * k // n_tiles, w.shape[1]), functools.partial(lambda i, k: (i // k, 0), k=k))
                  for w, k in zip(ffn_w, slab_steps)]
    assert all((w.shape[0] * k) % (n_tiles * BF16_ROWS) == 0 for w, k in zip(ffn_w, slab_steps))
    x1, w_up_bf16, w_down_bf16 = pl.pallas_call(
        _mixer_kernel,
        grid=(n_tiles,),
        in_specs=[pl.BlockSpec(memory_space=pltpu.SMEM), tile_spec] + [_resident(a.shape) for a in mixer_inputs] + slab_specs,
        out_specs=[tile_spec] + slab_specs,
        out_shape=[jax.ShapeDtypeStruct((seq, D_MODEL), f32)] + [jax.ShapeDtypeStruct(w.shape, bf16) for w in ffn_w],
        scratch_shapes=[
            pltpu.VMEM((ts, D_MODEL), bf16),
            pltpu.VMEM((POOL_TAIL + ts, POOL_WIDTH), f32),
            pltpu.VMEM((2, ts, ATTN_WIDTH), bf16),
            pltpu.VMEM((KV_CARRY + ts, ATTN_WIDTH), bf16),
            pltpu.VMEM((KV_CARRY + ts, 2 * ATTN_WIDTH), bf16),
            pltpu.VMEM((2, ts, MEM_WIDTH), bf16),
            pltpu.VMEM((ts, POOL_WIDTH), bf16),
            pltpu.VMEM((ts, ATTN_WIDTH), bf16),
            pltpu.VMEM((ts, MEM_WIDTH), bf16),
            pltpu.VMEM((ts, D_MODEL), bf16),
            pltpu.VMEM((ts, D_MODEL), f32),
            pltpu.VMEM((ts // MEM_ROWS, MEM_HEADS // 2, 2 * MEM_ROWS, mem_len), bf16),
            pltpu.VMEM((ATTN_HEADS // 2, 4 * CHUNK, WIN_KEYS), f32),
            pltpu.VMEM((ATTN_HEADS // 2, 4 * CHUNK, WIN_KEYS), f32),
            pltpu.VMEM((ATTN_HEADS // 2, 4 * CHUNK, WIN_KEYS), f32),
            pltpu.VMEM((ATTN_HEADS // 2, 4 * CHUNK, LANES), f32),
            pltpu.VMEM((ATTN_HEADS // 2, 4 * CHUNK, LANES), f32),
            pltpu.VMEM((ATTN_HEADS // 2, 4 * CHUNK, WIN_KEYS), bf16),
            pltpu.VMEM((ATTN_HEADS // 2, 4 * CHUNK, WIN_KEYS), bf16),
        ],
        compiler_params=params,
        name="mixer",
    )(jnp.zeros((1,), jnp.int32), x[0], *mixer_inputs, *ffn_w)

    ffn_inputs = (
        norm_ffn_g.astype(f32), w_up_bf16, conv_w[0].astype(f32), conv_b.astype(f32),
        w_down_bf16, norm_final_g.reshape(1, -1).astype(f32),
    )
    tf = FFN_TILE
    ffn_tile_spec = pl.BlockSpec((tf, D_MODEL), lambda i: (i, 0))
    out = pl.pallas_call(
        _ffn_kernel,
        grid=(seq // tf,),
        in_specs=[ffn_tile_spec] + [_resident(a.shape) for a in ffn_inputs],
        out_specs=ffn_tile_spec,
        out_shape=jax.ShapeDtypeStruct((seq, D_MODEL), x.dtype),
        scratch_shapes=[
            pltpu.VMEM((tf, D_MODEL), bf16),
            pltpu.VMEM((CONV_PAD + FFN_SUB, FF_BLOCK), f32),
            pltpu.VMEM((CONV_PAD, 2 * D_FF), f32),
            pltpu.VMEM((tf, D_FF), bf16),
        ],
        compiler_params=params,
        name="ffn",
    )(x1, *ffn_inputs)
    return out[None]
```

```python
import functools

import jax
import jax.numpy as jnp
from jax import lax
from jax.experimental import pallas as pl
from jax.experimental.pallas import tpu as pltpu

D_MODEL = 1024
CHUNK = 64
HEAD_DIM = 64
POOL_WINDOWS = (2, 4, 8, 16)
POOL_GROUP_DIM = 64
POOL_WIDTH = len(POOL_WINDOWS) * POOL_GROUP_DIM
ATTN_HEADS = 8
ATTN_WIDTH = ATTN_HEADS * HEAD_DIM
BAND_CHUNKS = 9
BAND_KEYS = BAND_CHUNKS * CHUNK
MAX_REL = 128
MEM_HEADS = 4
MEM_WIDTH = MEM_HEADS * HEAD_DIM
N_BRANCH = 3
D_FF = 2816
CONV_WIDTH = 3
RMS_EPS = 1e-6
NEG_INF = -1e30

O_POOL = 0
O_Q = O_POOL + POOL_WIDTH
O_K = O_Q + ATTN_WIDTH
O_V = O_K + ATTN_WIDTH
O_QM = O_V + ATTN_WIDTH
O_GATE = O_QM + MEM_WIDTH

LANES = 128
SUBLANES = 8
BF16_ROWS = 2 * SUBLANES
PAIR = 2 * HEAD_DIM
POOL_TAIL = 16
WIN_KEYS = BAND_KEYS + CHUNK
KV_CARRY = BAND_KEYS - CHUNK
SEQ_TILE = 512
FFN_TILE = 1024
FFN_SUB = 1024
MEM_ROWS = 512
PREP_STEPS = 8
FF_BLOCK = 256
CONV_PAD = SUBLANES
VMEM_LIMIT = 60 * 1024 * 1024


def _rms_norm(x, g):
    y = x * lax.rsqrt(jnp.mean(x * x, axis=-1, keepdims=True) + RMS_EPS)
    return y * g


def _dot(a, b):
    return jnp.dot(a, b, preferred_element_type=jnp.float32)


def _dot_nt(a, b):
    return lax.dot_general(a, b, (((1,), (1,)), ((), ())), preferred_element_type=jnp.float32)


def _prep_kernel(mem_ref, g_ref, w_kv_ref, *refs):
    n_w = (len(refs) - 2) // 2
    k_ref, v_ref = refs[n_w:n_w + 2]
    for src, dst in zip(refs[:n_w], refs[n_w + 2:]):
        dst[...] = src[...].astype(jnp.bfloat16)

    @pl.when(pl.program_id(0) == 0)
    def _first():
        mem_len = mem_ref.shape[0]
        mem_n = _rms_norm(mem_ref[...], g_ref[...]).astype(jnp.bfloat16)
        kv = _dot(mem_n, w_kv_ref[...].astype(jnp.bfloat16))
        k_ref[...] = kv[:, :MEM_WIDTH].astype(jnp.bfloat16)
        for p in range(MEM_HEADS // 2):
            v_ref[p, :, 0:PAIR] = kv[:, MEM_WIDTH + p * PAIR:MEM_WIDTH + (p + 1) * PAIR].astype(jnp.bfloat16)
            v_ref[p, :, PAIR:2 * PAIR] = jnp.ones((mem_len, PAIR), jnp.bfloat16)


def _mixer_kernel(zero_ref, x_ref, g_ref, w_in_ref, b_gate_ref, w_pool_ref, pool_scale_ref, rel_vec_ref,
                  km_ref, vm_ref, w_up_pool_ref, w_up_attn_ref, w_up_mem_ref, w_out_ref,
                  w_ffn_up_ref, w_ffn_down_ref,
                  o_ref, w_ffn_up_bf16_ref, w_ffn_down_bf16_ref,
                  h_buf, u_buf, q_buf, k_buf, v_buf, qm_buf, pool_buf, attn_buf, mem_buf, merged_buf,
                  part_buf, pm_buf, bias_buf, s_buf0, s_buf1, m_buf0, m_buf1, p_buf0, p_buf1):
    s_buf, m_buf, p_buf = (s_buf0, s_buf1), (m_buf0, m_buf1), (p_buf0, p_buf1)
    ts = x_ref.shape[0]
    tile = pl.program_id(0)

    @pl.when(tile == 0)
    def _init():
        u_buf[0:POOL_TAIL, :] = jnp.zeros((POOL_TAIL, POOL_WIDTH), jnp.float32)
        k_buf[0:KV_CARRY, :] = jnp.zeros((KV_CARRY, ATTN_WIDTH), jnp.bfloat16)
        ext_lane = lax.broadcasted_iota(jnp.int32, v_buf.shape, 1) % (2 * PAIR)
        v_buf[...] = jnp.where(ext_lane >= PAIR, 1.0, 0.0).astype(jnp.bfloat16)
        win_lane = lax.broadcasted_iota(jnp.int32, (ATTN_HEADS, WIN_KEYS), 1)
        vecs = rel_vec_ref[...]
        for qi in range(CHUNK):
            first = jnp.where(win_lane >= BAND_KEYS, NEG_INF, pltpu.roll(vecs, qi + BAND_KEYS, axis=1))
            second = jnp.where(win_lane < CHUNK, NEG_INF, pltpu.roll(vecs, qi, axis=1) if qi else vecs)
            for h in range(ATTN_HEADS):
                r = (h % 2) * 2 * CHUNK + qi
                bias_buf[h // 2, r:r + 1, :] = first[h:h + 1, :]
                bias_buf[h // 2, r + CHUNK:r + CHUNK + 1, :] = second[h:h + 1, :]

    w_ffn_up_bf16_ref[...] = w_ffn_up_ref[...].astype(jnp.bfloat16)
    w_ffn_down_bf16_ref[...] = w_ffn_down_ref[...].astype(jnp.bfloat16)

    h_buf[...] = _rms_norm(x_ref[...], g_ref[...]).astype(jnp.bfloat16)

    u_buf[POOL_TAIL:POOL_TAIL + ts, :] = _dot(h_buf[...], w_in_ref[:, O_POOL:O_Q])
    def split_heads(q, even_ref, odd_ref):
        odd = (lax.broadcasted_iota(jnp.int32, q.shape, 1) // HEAD_DIM) % 2 == 1
        q = q * (HEAD_DIM ** -0.5)
        even_ref[...] = jnp.where(odd, 0.0, q).astype(jnp.bfloat16)
        odd_ref[...] = jnp.where(odd, q, 0.0).astype(jnp.bfloat16)

    split_heads(_dot(h_buf[...], w_in_ref[:, O_Q:O_K]), q_buf.at[0], q_buf.at[1])
    k_buf[KV_CARRY:KV_CARRY + ts, :] = _dot(h_buf[...], w_in_ref[:, O_K:O_V]).astype(jnp.bfloat16)
    v = _dot(h_buf[...], w_in_ref[:, O_V:O_QM]).astype(jnp.bfloat16)
    for p in range(ATTN_HEADS // 2):
        v_buf[KV_CARRY:KV_CARRY + ts, p * 2 * PAIR:p * 2 * PAIR + PAIR] = v[:, p * PAIR:(p + 1) * PAIR]
    split_heads(_dot(h_buf[...], w_in_ref[:, O_QM:O_GATE]), qm_buf.at[0], qm_buf.at[1])

    u = u_buf[POOL_TAIL:POOL_TAIL + ts, :]
    lane_group = lax.broadcasted_iota(jnp.int32, (ts, POOL_WIDTH), 1) // POOL_GROUP_DIM
    run = u
    win_sum = jnp.zeros_like(u)
    for j in range(1, POOL_TAIL + 1):
        if j in POOL_WINDOWS:
            win_sum = jnp.where(lane_group == POOL_WINDOWS.index(j), run, win_sum)
        if j < POOL_TAIL:
            run = run + u_buf[POOL_TAIL - j:POOL_TAIL - j + ts, :]
    window = jnp.left_shift(2, lane_group)
    pos1 = tile * ts + lax.broadcasted_iota(jnp.int32, (ts, POOL_WIDTH), 0) + 1
    cnt = jnp.minimum(pos1, window).astype(jnp.float32)
    pooled = (win_sum / cnt - u).astype(jnp.bfloat16)
    pool_buf[...] = (_dot(pooled, w_pool_ref[...]) * pool_scale_ref[...]).astype(jnp.bfloat16)
    u_buf[0:POOL_TAIL, :] = u_buf[ts:ts + POOL_TAIL, :]

    def pair_cols(h):
        return slice((h // 2) * PAIR, (h // 2 + 1) * PAIR)

    staged = zero_ref[0]

    def pair_output(acc):
        half = acc.shape[0] // 2
        out = acc[:, :PAIR] / acc[:, PAIR:]
        lane_is_odd = lax.broadcasted_iota(jnp.int32, (half, PAIR), 1) >= HEAD_DIM
        return jnp.where(lane_is_odd, out[half:], out[:half]).astype(jnp.bfloat16)

    mem_len = km_ref.shape[0]
    n_mem = ts // MEM_ROWS

    def mem_score(r):
        r0 = r * MEM_ROWS
        for h in range(MEM_HEADS):
            s = _dot_nt(qm_buf[h % 2, r0:r0 + MEM_ROWS, pair_cols(h)], km_ref[:, pair_cols(h)])
            e = jnp.exp(s - jnp.max(s, axis=-1, keepdims=True))
            pm_buf[r, h // 2, (h % 2) * MEM_ROWS:(h % 2 + 1) * MEM_ROWS, :] = e.astype(jnp.bfloat16)

    def mem_value(r):
        r0 = r * MEM_ROWS
        for p in range(MEM_HEADS // 2):
            acc = _dot(pm_buf[staged + r, p], vm_ref[p])
            mem_buf[r0:r0 + MEM_ROWS, pair_cols(2 * p)] = pair_output(acc)

    merge_blk = 2 * LANES
    n_merge = D_MODEL // merge_blk

    def merge_cols(nb):
        return slice(nb * merge_blk, (nb + 1) * merge_blk)

    def gated(b, branch_buf, w_up_ref, nb):
        lo = b * D_MODEL + nb * merge_blk
        logits = _dot(h_buf[...], w_in_ref[:, O_GATE + lo:O_GATE + lo + merge_blk]) + b_gate_ref[:, lo:lo + merge_blk]
        return jax.nn.sigmoid(logits) * _dot(branch_buf[...], w_up_ref[:, merge_cols(nb)])

    win_col = lax.broadcasted_iota(jnp.int32, (1, WIN_KEYS), 1)
    pad_rows = jnp.where(tile == 0, KV_CARRY, 0)

    def ext_cols(p):
        return slice(p * 2 * PAIR, (p + 1) * 2 * PAIR)

    n_k = WIN_KEYS // LANES
    n_cp = ts // (2 * CHUNK)

    def score_phase(cp):
        base = cp * 2 * CHUNK
        slot = cp % 2
        pen = jnp.where(win_col + base < pad_rows, NEG_INF, 0.0)
        for p in range(ATTN_HEADS // 2):
            q_rows = jnp.concatenate([q_buf[0, base:base + 2 * CHUNK, pair_cols(2 * p)],
                                      q_buf[1, base:base + 2 * CHUNK, pair_cols(2 * p)]], axis=0)
            s = _dot_nt(q_rows, k_buf[base:base + WIN_KEYS, pair_cols(2 * p)]) + bias_buf[p] + pen
            s_buf[slot][p] = s
            part = s[:, 0:LANES]
            for k in range(1, n_k):
                part = jnp.maximum(part, s[:, k * LANES:(k + 1) * LANES])
            m_buf[slot][p] = jnp.broadcast_to(jnp.max(part, axis=-1, keepdims=True), (4 * CHUNK, LANES))

    def softmax_phase(cp):
        slot = cp % 2
        for p in range(ATTN_HEADS // 2):
            m = m_buf[slot][staged + p]
            for k in range(n_k):
                cols = slice(k * LANES, (k + 1) * LANES)
                p_buf[slot][p, :, cols] = jnp.exp(s_buf[slot][staged + p, :, cols] - m).astype(jnp.bfloat16)

    def value_phase(cp):
        base = cp * 2 * CHUNK
        slot = cp % 2
        for p in range(ATTN_HEADS // 2):
            acc = _dot(p_buf[slot][staged + p], v_buf[base:base + WIN_KEYS, ext_cols(p)])
            attn_buf[base:base + 2 * CHUNK, pair_cols(2 * p)] = pair_output(acc)

    mem_steps = min(2, n_mem)
    mem_per_step = n_mem // mem_steps
    assert mem_steps + n_merge <= n_cp + 2
    for step in range(n_cp + 2):
        mem_blocks = range(step * mem_per_step, (step + 1) * mem_per_step) if step < mem_steps else ()
        for r in mem_blocks:
            mem_score(r)
        if step < n_cp:
            score_phase(step)
        if 0 <= step - mem_steps < n_merge:
            nb = step - mem_steps
            part_buf[:, merge_cols(nb)] = (gated(0, pool_buf, w_up_pool_ref, nb)
                                           + gated(2, mem_buf, w_up_mem_ref, nb))
        if 0 <= step - 1 < n_cp:
            softmax_phase(step - 1)
        if 0 <= step - 2 < n_cp:
            value_phase(step - 2)
        for r in mem_blocks:
            mem_value(r)

    k_buf[0:KV_CARRY, :] = k_buf[ts:ts + KV_CARRY, :]
    for p in range(ATTN_HEADS // 2):
        vcols = slice(p * 2 * PAIR, p * 2 * PAIR + PAIR)
        v_buf[0:KV_CARRY, vcols] = v_buf[ts:ts + KV_CARRY, vcols]

    for nb in range(n_merge):
        merged = part_buf[:, merge_cols(nb)] + gated(1, attn_buf, w_up_attn_ref, nb)
        merged_buf[:, merge_cols(nb)] = merged.astype(jnp.bfloat16)
    o_ref[...] = x_ref[...] + _dot(merged_buf[...], w_out_ref[...])


def _ffn_kernel(x_ref, g_ref, w_up_ref, conv_w_ref, conv_b_ref, w_down_ref, g_final_ref, o_ref,
                h_buf, a_buf, tail_buf, hid_buf):
    sub = a_buf.shape[0] - CONV_PAD

    @pl.when(pl.program_id(0) == 0)
    def _init():
        tail_buf[...] = jnp.zeros_like(tail_buf)

    def conv_up(rows, cols):
        a = _dot(h_buf[rows, :], w_up_ref[:, cols])
        a_buf[0:CONV_PAD, :] = tail_buf[:, cols]
        a_buf[CONV_PAD:CONV_PAD + sub, :] = a
        tail_buf[:, cols] = a_buf[sub:sub + CONV_PAD, :]
        out = conv_b_ref[:, cols] + a * conv_w_ref[CONV_WIDTH - 1:CONV_WIDTH, cols]
        for t in range(CONV_WIDTH - 1):
            back = CONV_WIDTH - 1 - t
            out = out + a_buf[CONV_PAD - back:CONV_PAD - back + sub, :] * conv_w_ref[t:t + 1, cols]
        return out

    for r0 in range(0, x_ref.shape[0], sub):
        rows = slice(r0, r0 + sub)
        h_buf[rows, :] = _rms_norm(x_ref[rows, :], g_ref[...]).astype(jnp.bfloat16)
        for jb in range(D_FF // FF_BLOCK):
            val = conv_up(rows, slice(D_FF + jb * FF_BLOCK, D_FF + (jb + 1) * FF_BLOCK))
            gate = conv_up(rows, slice(jb * FF_BLOCK, (jb + 1) * FF_BLOCK))
            act = 0.5 * gate * (1.0 + lax.erf(gate * (2.0 ** -0.5)))
            hid_buf[rows, jb * FF_BLOCK:(jb + 1) * FF_BLOCK] = (act * val).astype(jnp.bfloat16)
        for q0 in range(r0, r0 + sub, sub // 4):
            quarter = slice(q0, q0 + sub // 4)
            y = x_ref[quarter, :] + _dot(hid_buf[quarter, :], w_down_ref[...])
            o_ref[quarter, :] = _rms_norm(y, g_final_ref[...])


def _resident(shape):
    zeros = (0,) * len(shape)
    return pl.BlockSpec(shape, lambda i: zeros, pipeline_mode=pl.Buffered(1))


def _whole(shape):
    zeros = (0,) * len(shape)
    return pl.BlockSpec(shape, lambda i: zeros)


def _rel_vector(rel_bias):
    assert CHUNK - 1 <= MAX_REL <= WIN_KEYS - CHUNK - 1
    n_clipped = WIN_KEYS - CHUNK - MAX_REL
    far = jnp.broadcast_to(rel_bias[:, 2 * MAX_REL:], (ATTN_HEADS, n_clipped))
    near = rel_bias[:, MAX_REL - (CHUNK - 1):2 * MAX_REL][:, ::-1]
    unused = jnp.zeros((ATTN_HEADS, 1), rel_bias.dtype)
    return jnp.concatenate([unused, far, near], axis=1).astype(jnp.float32)


def _block_diag(w):
    g, c, _ = w.shape
    eye = jnp.eye(g, dtype=w.dtype)
    return (eye[:, None, :, None] * w[:, :, None, :]).reshape(g * c, g * c)


def kernel(x, mem, norm_mix_g, norm_mem_g, w_in, b_gate, w_pool, pool_scale, rel_bias, w_mem_kv,
           w_up_pool, w_up_attn, w_up_mem, w_out, norm_ffn_g, w_ffn_up, conv_w, conv_b, w_ffn_down,
           norm_final_g):
    batch, seq, d = x.shape
    assert batch == 1 and d == D_MODEL and seq % SEQ_TILE == 0
    assert norm_mix_g.shape[0] == 1, "single layer"
    bf16 = jnp.bfloat16
    f32 = jnp.float32
    ts = SEQ_TILE
    n_tiles = seq // ts
    mem_len = mem.shape[1]
    params = pltpu.CompilerParams(dimension_semantics=("arbitrary",), vmem_limit_bytes=VMEM_LIMIT)

    mixer_w = (w_in[0], w_up_pool[0], w_up_attn[0], w_up_mem[0], w_out[0], _block_diag(w_pool[0]))
    prep_in = (mem[0], norm_mem_g.astype(f32), w_mem_kv[0])
    assert all(w.shape[0] % (PREP_STEPS * BF16_ROWS) == 0 for w in mixer_w)
    w_slabs = [pl.BlockSpec((w.shape[0] // PREP_STEPS, w.shape[1]), lambda i: (i, 0)) for w in mixer_w]
    prep_out = pl.pallas_call(
        _prep_kernel,
        grid=(PREP_STEPS,),
        in_specs=[_resident(a.shape) for a in prep_in] + w_slabs,
        out_specs=[_whole((mem_len, MEM_WIDTH)), _whole((MEM_HEADS // 2, mem_len, 2 * PAIR))] + w_slabs,
        out_shape=[jax.ShapeDtypeStruct((mem_len, MEM_WIDTH), bf16),
                   jax.ShapeDtypeStruct((MEM_HEADS // 2, mem_len, 2 * PAIR), bf16)]
                  + [jax.ShapeDtypeStruct(w.shape, bf16) for w in mixer_w],
        compiler_params=params,
        name="prep",
    )(*prep_in, *mixer_w)
    km, vm, w_in_bf16, w_up_pool_bf16, w_up_attn_bf16, w_up_mem_bf16, w_out_bf16, w_pool_bf16 = prep_out

    tile_spec = pl.BlockSpec((ts, D_MODEL), lambda i: (i, 0))
    mixer_inputs = (
        norm_mix_g.astype(f32), w_in_bf16, b_gate.astype(f32), w_pool_bf16,
        pool_scale.astype(f32), _rel_vector(rel_bias[0]), km, vm,
        w_up_pool_bf16, w_up_attn_bf16, w_up_mem_bf16, w_out_bf16,
    )
    ffn_w = (w_ffn_up[0], w_ffn_down[0])
    slab_steps = (1, 2)
    slab_specs = [pl.BlockSpec((w.shape[0] * k // n_tiles, w.shape[1]), functools.partial(lambda i, k: (i // k, 0), k=k))
                  for w, k in zip(ffn_w, slab_steps)]
    assert all((w.shape[0] * k) % (n_tiles * BF16_ROWS) == 0 for w, k in zip(ffn_w, slab_steps))
    x1, w_up_bf16, w_down_bf16 = pl.pallas_call(
        _mixer_kernel,
        grid=(n_tiles,),
        in_specs=[pl.BlockSpec(memory_space=pltpu.SMEM), tile_spec] + [_resident(a.shape) for a in mixer_inputs] + slab_specs,
        out_specs=[tile_spec] + slab_specs,
        out_shape=[jax.ShapeDtypeStruct((seq, D_MODEL), f32)] + [jax.ShapeDtypeStruct(w.shape, bf16) for w in ffn_w],
        scratch_shapes=[
            pltpu.VMEM((ts, D_MODEL), bf16),
            pltpu.VMEM((POOL_TAIL + ts, POOL_WIDTH), f32),
            pltpu.VMEM((2, ts, ATTN_WIDTH), bf16),
            pltpu.VMEM((KV_CARRY + ts, ATTN_WIDTH), bf16),
            pltpu.VMEM((KV_CARRY + ts, 2 * ATTN_WIDTH), bf16),
            pltpu.VMEM((2, ts, MEM_WIDTH), bf16),
            pltpu.VMEM((ts, POOL_WIDTH), bf16),
            pltpu.VMEM((ts, ATTN_WIDTH), bf16),
            pltpu.VMEM((ts, MEM_WIDTH), bf16),
            pltpu.VMEM((ts, D_MODEL), bf16),
            pltpu.VMEM((ts, D_MODEL), f32),
            pltpu.VMEM((ts // MEM_ROWS, MEM_HEADS // 2, 2 * MEM_ROWS, mem_len), bf16),
            pltpu.VMEM((ATTN_HEADS // 2, 4 * CHUNK, WIN_KEYS), f32),
            pltpu.VMEM((ATTN_HEADS // 2, 4 * CHUNK, WIN_KEYS), f32),
            pltpu.VMEM((ATTN_HEADS // 2, 4 * CHUNK, WIN_KEYS), f32),
            pltpu.VMEM((ATTN_HEADS // 2, 4 * CHUNK, LANES), f32),
            pltpu.VMEM((ATTN_HEADS // 2, 4 * CHUNK, LANES), f32),
            pltpu.VMEM((ATTN_HEADS // 2, 4 * CHUNK, WIN_KEYS), bf16),
            pltpu.VMEM((ATTN_HEADS // 2, 4 * CHUNK, WIN_KEYS), bf16),
        ],
        compiler_params=params,
        name="mixer",
    )(jnp.zeros((1,), jnp.int32), x[0], *mixer_inputs, *ffn_w)

    ffn_inputs = (
        norm_ffn_g.astype(f32), w_up_bf16, conv_w[0].astype(f32), conv_b.astype(f32),
        w_down_bf16, norm_final_g.reshape(1, -1).astype(f32),
    )
    tf = FFN_TILE
    ffn_tile_spec = pl.BlockSpec((tf, D_MODEL), lambda i: (i, 0))
    out = pl.pallas_call(
        _ffn_kernel,
        grid=(seq // tf,),
        in_specs=[ffn_tile_spec] + [_resident(a.shape) for a in ffn_inputs],
        out_specs=ffn_tile_spec,
        out_shape=jax.ShapeDtypeStruct((seq, D_MODEL), x.dtype),
        scratch_shapes=[
            pltpu.VMEM((tf, D_MODEL), bf16),
            pltpu.VMEM((CONV_PAD + FFN_SUB, FF_BLOCK), f32),
            pltpu.VMEM((CONV_PAD, 2 * D_FF), f32),
            pltpu.VMEM((tf, D_FF), bf16),
        ],
        compiler_params=params,
        name="ffn",
    )(x1, *ffn_inputs)
    return out[None]
```
